```python
import math
import jax
import jax.numpy as jnp
from jax import lax
import numpy as np

D_MODEL = 1024
BATCH = 8
SEQ = 4096
DEPTH = 4

GRID_W = 64
CTX_LEN = 256
EPS = 1e-6

DA_HEADS = 4
DA_HEAD_DIM = 64
DA_V_DIM = 2 * DA_HEAD_DIM
DA_WIDTH = DA_HEADS * 2 * DA_HEAD_DIM
ROPE_AXIS_DIM = DA_HEAD_DIM // 2
ROPE_THETA = 10000.0
Q_BLOCK = 128

CM_CHUNK = 128
CM_GROUPS = 4
CM_WIDTH = 512
CM_GROUP_W = CM_WIDTH // CM_GROUPS

FN_GROUPS = 4
FN_WIDTH = 512
FN_GROUP_W = FN_WIDTH // FN_GROUPS

N_BRANCH = 3
BRANCH_W = 512
SPLIT_POINTS = [DA_WIDTH, 2 * DA_WIDTH, 3 * DA_WIDTH, 3 * DA_WIDTH + CM_WIDTH,
                3 * DA_WIDTH + 2 * CM_WIDTH, 3 * DA_WIDTH + 2 * CM_WIDTH + FN_WIDTH]
IN_COLS = SPLIT_POINTS[-1] + N_BRANCH * D_MODEL

PEER_HEADS = 8
PEER_KEYS = 128
PEER_EXPERTS = PEER_KEYS * PEER_KEYS
PEER_QDIM = 256
PEER_HALF = PEER_QDIM // 2
PEER_TOPK = 16
PEER_TOKEN_BLOCK = 128

kernel_name = 'hybrid_diffattn_chunkmlp_fourier_peer_dit'

F32 = jnp.float32


def rmsnorm(x, g):
    xf = x.astype(F32)
    y = xf * lax.rsqrt(jnp.mean(xf * xf, axis=-1, keepdims=True) + EPS)
    return (y * g.astype(F32)).astype(x.dtype)


def rms_unit(x):
    xf = x.astype(F32)
    return (xf * lax.rsqrt(jnp.mean(xf * xf, axis=-1, keepdims=True) + EPS)).astype(x.dtype)


def layernorm(x, g):
    xf = x.astype(F32)
    mu = jnp.mean(xf, axis=-1, keepdims=True)
    var = jnp.mean(jnp.square(xf - mu), axis=-1, keepdims=True)
    return ((xf - mu) * lax.rsqrt(var + EPS) * g.astype(F32)).astype(x.dtype)


def modulate(x, shift, scale):
    return x * (1 + scale) + shift


def axial_angles(rows):
    r, col = jnp.meshgrid(jnp.arange(rows, dtype=F32), jnp.arange(GRID_W, dtype=F32), indexing='ij')
    freqs = ROPE_THETA ** (-jnp.arange(0, ROPE_AXIS_DIM, 2, dtype=F32) / ROPE_AXIS_DIM)
    ang_r = r.reshape(-1, 1) * freqs
    ang_c = col.reshape(-1, 1) * freqs
    return ang_r[:, None, None, :], ang_c[:, None, None, :]


def rope_axis(xp, ang):
    x1, x2 = jnp.split(xp, 2, axis=-1)
    cos, sin = jnp.cos(ang), jnp.sin(ang)
    return jnp.concatenate([x1 * cos - x2 * sin, x2 * cos + x1 * sin], axis=-1)


def apply_axial_rope(x, ang_r, ang_c):
    xf = x.astype(F32)
    out = jnp.concatenate([rope_axis(xf[..., :ROPE_AXIS_DIM], ang_r),
                           rope_axis(xf[..., ROPE_AXIS_DIM:], ang_c)], axis=-1)
    return out.astype(x.dtype)


def split_cols(p):
    B, L, _ = p.shape
    q, k, v, zu, zv, zf, gl = jnp.split(p, SPLIT_POINTS, axis=-1)
    return (q.reshape(B, L, DA_HEADS, 2, DA_HEAD_DIM),
            k.reshape(B, L, DA_HEADS, 2, DA_HEAD_DIM),
            v.reshape(B, L, DA_HEADS, DA_V_DIM),
            zu, zv, zf,
            gl.reshape(B, L, N_BRANCH, D_MODEL))


def diff_attend(q, k, v, lam):
    s = jnp.einsum('bqhmd,bkhmd->bhmqk', q, k).astype(F32) * (DA_HEAD_DIM ** -0.5)
    a = jax.nn.softmax(s, axis=-1)
    w = (a[:, :, 0] - lam * a[:, :, 1]).astype(v.dtype)
    return jnp.einsum('bhqk,bkhe->bqhe', w, v)


def latent_diff_attention(q, k_all, v_all, lam):
    B, L = q.shape[0], q.shape[1]
    nb = L // Q_BLOCK
    qb = jnp.moveaxis(q.reshape(B, nb, Q_BLOCK, DA_HEADS, 2, DA_HEAD_DIM), 1, 0)
    ob = lax.map(lambda qq: diff_attend(qq, k_all, v_all, lam), qb)
    return jnp.moveaxis(ob, 0, 1).reshape(B, L, DA_HEADS, DA_V_DIM)


def chunk_mlp(zu, zv, ln_g, w_s, b_s):
    B, L, _ = zu.shape
    u = jax.nn.gelu(zu, approximate=False)
    v = layernorm(jax.nn.gelu(zv, approximate=False), ln_g)
    vc = v.reshape(B, L // CM_CHUNK, CM_CHUNK, CM_GROUPS, CM_GROUP_W)
    s = jnp.einsum('gpq,bnqgc->bnpgc', w_s, vc) + b_s.T[:, :, None]
    return u * s.reshape(B, L, CM_WIDTH)


def fourier_mix(z):
    B, L, _ = z.shape
    zg = z.astype(F32).reshape(B, L, FN_GROUPS, FN_GROUP_W)
    return jnp.fft.fftn(zg, axes=(1, 3), norm='ortho').real.reshape(B, L, FN_WIDTH).astype(z.dtype)


def mixer_merge(att, zu, zv, zf, gl, lam_init, subln_g, cm_ln_g, cm_w_s, cm_b_s, w_branch, b_gate, w_out):
    B, L = att.shape[0], att.shape[1]
    a = (rmsnorm(att, subln_g) * (1.0 - lam_init)).reshape(B, L, DA_WIDTH)
    m = chunk_mlp(zu, zv, cm_ln_g, cm_w_s, cm_b_s)
    f = fourier_mix(zf)
    br = jnp.stack([a, m, f], axis=2)
    y = jnp.einsum('blnw,nwd->blnd', br, w_branch)
    g = jax.nn.sigmoid(gl + b_gate.reshape(N_BRANCH, D_MODEL))
    return jnp.sum(g * y, axis=2) @ w_out


def peer_ffn(h, w_q, sub_keys, u_tab, v_tab):
    B, L, D = h.shape
    hb = h.reshape(-1, PEER_TOKEN_BLOCK, D)

    def block(t):
        T = t.shape[0]
        q = rms_unit((t @ w_q).reshape(T, PEER_HEADS, 2, PEER_HALF))
        s = jnp.einsum('thpc,hpkc->thpk', q, sub_keys).astype(F32)
        s1, i1 = lax.top_k(s[:, :, 0], PEER_TOPK)
        s2, i2 = lax.top_k(s[:, :, 1], PEER_TOPK)
        cand_s = (s1[..., :, None] + s2[..., None, :]).reshape(T, PEER_HEADS, PEER_TOPK * PEER_TOPK)
        cand_i = (i1[..., :, None] * PEER_KEYS + i2[..., None, :]).reshape(T, PEER_HEADS, PEER_TOPK * PEER_TOPK)
        top_s, pos = lax.top_k(cand_s, PEER_TOPK)
        idx = jnp.take_along_axis(cand_i, pos, axis=-1)
        g = jax.nn.softmax(top_s, axis=-1)
        act = jax.nn.gelu(jnp.einsum('td,thkd->thk', t, u_tab[idx]).astype(F32), approximate=False)
        return jnp.einsum('thk,thkd->td', (g * act).astype(t.dtype), v_tab[idx])

    return lax.map(block, hb).reshape(B, L, D)


def setup_inputs(seed: int = 0) -> dict:
    key = jax.random.key(seed)
    ks = jax.random.split(key, 23)

    def nrm(k, shape, scale):
        return jax.random.normal(k, shape, F32) * scale

    D = D_MODEL
    return {
        'x': nrm(ks[0], (BATCH, SEQ, D), 1.0),
        'c': nrm(ks[1], (BATCH, D), 1.0),
        'ctx': nrm(ks[2], (BATCH, CTX_LEN, D), 1.0),
        'c_ctx': nrm(ks[3], (D,), 1.0),
        'w_ada': nrm(ks[4], (DEPTH, D, 6 * D), 0.5 * D ** -0.5),
        'b_ada': nrm(ks[5], (DEPTH, 6 * D), 0.02),
        'norm1_g': 1.0 + nrm(ks[6], (DEPTH, D), 0.02),
        'norm2_g': 1.0 + nrm(ks[7], (DEPTH, D), 0.02),
        'w_in': nrm(ks[8], (DEPTH, D, IN_COLS), D ** -0.5),
        'b_gate': nrm(ks[9], (DEPTH, N_BRANCH * D), 0.02),
        'q_norm_g': 1.0 + nrm(ks[10], (DEPTH, DA_HEAD_DIM), 0.02),
        'k_norm_g': 1.0 + nrm(ks[11], (DEPTH, DA_HEAD_DIM), 0.02),
        'lam_params': nrm(ks[12], (DEPTH, 4, DA_HEAD_DIM), 0.1),
        'subln_g': 1.0 + nrm(ks[13], (DEPTH, DA_V_DIM), 0.02),
        'cm_ln_g': 1.0 + nrm(ks[14], (DEPTH, CM_WIDTH), 0.02),
        'cm_w_s': nrm(ks[15], (DEPTH, CM_GROUPS, CM_CHUNK, CM_CHUNK), CM_CHUNK ** -0.5),
        'cm_b_s': 1.0 + nrm(ks[16], (DEPTH, CM_GROUPS, CM_CHUNK), 0.02),
        'w_branch': nrm(ks[17], (DEPTH, N_BRANCH, BRANCH_W, D), BRANCH_W ** -0.5),
        'w_out': nrm(ks[18], (DEPTH, D, D), D ** -0.5),
        'peer_w_q': nrm(ks[19], (DEPTH, D, PEER_HEADS * PEER_QDIM), D ** -0.5),
        'peer_sub_keys': nrm(ks[20], (DEPTH, PEER_HEADS, 2, PEER_KEYS, PEER_HALF), PEER_HALF ** -0.5),
        'peer_u': nrm(ks[21], (DEPTH, PEER_EXPERTS, D), D ** -0.5),
        'peer_v': nrm(ks[22], (DEPTH, PEER_EXPERTS, D), PEER_HEADS ** -0.5),
    }


def reference(x, c, ctx, c_ctx, w_ada, b_ada, norm1_g, norm2_g, w_in, b_gate, q_norm_g, k_norm_g,
              lam_params, subln_g, cm_ln_g, cm_w_s, cm_b_s, w_branch, w_out,
              peer_w_q, peer_sub_keys, peer_u, peer_v):
    B, L, D = x.shape
    Lc = ctx.shape[1]
    rows = L // GRID_W
    ang_r, ang_c = axial_angles(rows)
    c_act = jax.nn.silu(c)
    cc_act = jax.nn.silu(c_ctx)
    xc = ctx
    for l in range(DEPTH):
        last = l == DEPTH - 1
        lam_init = 0.8 - 0.6 * math.exp(-0.3 * l)
        lp = lam_params[l].astype(F32)
        lam = jnp.exp(jnp.sum(lp[0] * lp[1])) - jnp.exp(jnp.sum(lp[2] * lp[3])) + lam_init
        mod = (c_act @ w_ada[l] + b_ada[l]).reshape(B, 6, 1, D)
        mod_c = (cc_act @ w_ada[l] + b_ada[l]).reshape(6, 1, D)

        h = modulate(rmsnorm(x, norm1_g[l]), mod[:, 0], mod[:, 1])
        hc = modulate(rmsnorm(xc, norm1_g[l]), mod_c[0], mod_c[1])
        q, k, v, zu, zv, zf, gl = split_cols(h @ w_in[l])
        q = apply_axial_rope(rmsnorm(q, q_norm_g[l]), ang_r, ang_c)
        k = apply_axial_rope(rmsnorm(k, k_norm_g[l]), ang_r, ang_c)
        if last:
            kvc = hc @ w_in[l][:, SPLIT_POINTS[0]:SPLIT_POINTS[2]]
            kc = kvc[..., :DA_WIDTH].reshape(B, Lc, DA_HEADS, 2, DA_HEAD_DIM)
            vc = kvc[..., DA_WIDTH:].reshape(B, Lc, DA_HEADS, DA_V_DIM)
        else:
            qc, kc, vc, zuc, zvc, zfc, glc = split_cols(hc @ w_in[l])
        kc = rmsnorm(kc, k_norm_g[l])
        att = latent_diff_attention(q, jnp.concatenate([kc, k], axis=1),
                                    jnp.concatenate([vc, v], axis=1), lam)
        x = x + mod[:, 2] * mixer_merge(att, zu, zv, zf, gl, lam_init, subln_g[l], cm_ln_g[l], cm_w_s[l],
                                        cm_b_s[l], w_branch[l], b_gate[l], w_out[l])
        if not last:
            attc = diff_attend(rmsnorm(qc, q_norm_g[l]), kc, vc, lam)
            xc = xc + mod_c[2] * mixer_merge(attc, zuc, zvc, zfc, glc, lam_init, subln_g[l], cm_ln_g[l],
                                             cm_w_s[l], cm_b_s[l], w_branch[l], b_gate[l], w_out[l])

        h2 = modulate(rmsnorm(x, norm2_g[l]), mod[:, 3], mod[:, 4])
        x = x + mod[:, 5] * peer_ffn(h2, peer_w_q[l], peer_sub_keys[l], peer_u[l], peer_v[l])
        if not last:
            h2c = modulate(rmsnorm(xc, norm2_g[l]), mod_c[3], mod_c[4])
            xc = xc + mod_c[5] * peer_ffn(h2c, peer_w_q[l], peer_sub_keys[l], peer_u[l], peer_v[l])
    return x
```

```python
import functools
import math

import jax
import jax.numpy as jnp
from jax import lax
from jax.experimental import pallas as pl
from jax.experimental.pallas import tpu as pltpu

F32 = jnp.float32
BF16 = jnp.bfloat16
I32 = jnp.int32

LANES = 128
SUBLANES = 8
VMEM_BYTES = 64 * 1024 * 1024

EPS = 1e-6
GRID_W = 64
HEADS = 4
HEAD_DIM = 64
ROPE_AXIS_DIM = HEAD_DIM // 2
ROPE_THETA = 10000.0
BRANCH_W = 512
CM_CHUNK = 128
CM_GROUPS = 4
FN_GROUPS = 4
FN_GROUP_W = BRANCH_W // FN_GROUPS
PEER_HEADS = 8
PEER_KEYS = 128
PEER_TOPK = 16
PEER_SEL = PEER_HEADS * PEER_TOPK
ROW_WORDS = 4
HI_MASK = -65536


def _vmem_limit(nbytes):
    return int(min(VMEM_BYTES - 4 * 1024 * 1024, max(32 * 1024 * 1024, nbytes)))


def _params(sem, vmem=None):
    return pltpu.CompilerParams(dimension_semantics=sem,
                                vmem_limit_bytes=_vmem_limit(vmem or 0))


def _gelu(x):
    return 0.5 * x * (1.0 + lax.erf(x * (1.0 / math.sqrt(2.0))))


def _ada_kernel(c_ref, w_ref, b_ref, o_ref):
    c = c_ref[...]
    a = c * jax.nn.sigmoid(c)
    o_ref[0] = jnp.dot(a.astype(BF16), w_ref[0].astype(BF16), preferred_element_type=F32) + b_ref[0]


def ada_mod(cin, w_ada, b_ada):
    depth, d, n = w_ada.shape
    r = cin.shape[0]
    tn = 1536
    return pl.pallas_call(
        _ada_kernel,
        grid=(depth, n // tn),
        in_specs=[pl.BlockSpec((r, d), lambda l, j: (0, 0)),
                  pl.BlockSpec((1, d, tn), lambda l, j: (l, 0, j)),
                  pl.BlockSpec((1, 1, tn), lambda l, j: (l, 0, j))],
        out_specs=pl.BlockSpec((1, r, tn), lambda l, j: (l, 0, j)),
        out_shape=jax.ShapeDtypeStruct((depth, r, n), F32),
        compiler_params=_params(("arbitrary", "arbitrary")),
        name="ada_mod",
    )(cin, w_ada, b_ada.reshape(depth, 1, n))


def _normproj_kernel(x_ref, g_ref, sh_ref, sc_ref, w_ref, *refs, emit_h):
    if emit_h:
        h_ref, p_ref, hs_ref = refs
    else:
        p_ref, hs_ref = refs

    @pl.when(pl.program_id(2) == 0)
    def _():
        x = x_ref[0]
        y = x * lax.rsqrt(jnp.mean(x * x, axis=-1, keepdims=True) + EPS) * g_ref[...]
        h = y * (1.0 + sc_ref[...]) + sh_ref[...]
        hs_ref[...] = h.astype(BF16)
        if emit_h:
            h_ref[0] = h

    p_ref[0] = jnp.dot(hs_ref[...], w_ref[...], preferred_element_type=F32).astype(p_ref.dtype)


def normproj(x, gain, mod4, shift_i, scale_i, w, *, ctx, emit_h, out_dtype):
    b, l, d = x.shape
    n = w.shape[1]
    tm = min(512, l)
    tn = 1536 if n % 1536 == 0 else 1024
    assert l % tm == 0 and n % tn == 0
    row = (lambda bi: 8) if ctx else (lambda bi: bi)
    in_specs = [pl.BlockSpec((1, tm, d), lambda bi, i, j: (bi, i, 0)),
                pl.BlockSpec((1, d), lambda bi, i, j: (0, 0)),
                pl.BlockSpec((None, None, 1, d), lambda bi, i, j: (row(bi), shift_i, 0, 0)),
                pl.BlockSpec((None, None, 1, d), lambda bi, i, j: (row(bi), scale_i, 0, 0)),
                pl.BlockSpec((d, tn), lambda bi, i, j: (0, j))]
    out_specs = [pl.BlockSpec((1, tm, tn), lambda bi, i, j: (bi, i, j))]
    out_shape = [jax.ShapeDtypeStruct((b, l, n), out_dtype)]
    if emit_h:
        out_specs = [pl.BlockSpec((1, tm, d), lambda bi, i, j: (bi, i, 0))] + out_specs
        out_shape = [jax.ShapeDtypeStruct((b, l, d), F32)] + out_shape
    return pl.pallas_call(
        functools.partial(_normproj_kernel, emit_h=emit_h),
        grid=(b, l // tm, n // tn),
        in_specs=in_specs, out_specs=out_specs, out_shape=out_shape,
        scratch_shapes=[pltpu.VMEM((tm, d), BF16)],
        compiler_params=_params(("arbitrary", "arbitrary", "arbitrary")),
        name="normproj",
    )(x, gain.reshape(1, d), mod4, mod4, w)


def _qkprep_kernel(p_ref, g_ref, cos_ref, sin_ref, gm_ref, o_ref):
    x = p_ref[0].astype(F32)
    ms = jnp.dot((x * x).astype(BF16), gm_ref[...], preferred_element_type=F32)
    y = x * lax.rsqrt(ms + EPS) * g_ref[0]
    w = y.shape[-1]
    lane = lax.broadcasted_iota(I32, y.shape, 1)
    half = ROPE_AXIS_DIM // 2
    partner = jnp.where((lane % ROPE_AXIS_DIM) < half,
                        pltpu.roll(y, w - half, axis=1), pltpu.roll(y, half, axis=1))
    o_ref[0, 0] = (y * cos_ref[...] + partner * sin_ref[...]).astype(o_ref.dtype)


def qkprep(p, gains, cos, sin, gmean):
    b, l, _ = p.shape
    w = HEADS * 2 * HEAD_DIM
    tm = min(512, l)
    return pl.pallas_call(
        _qkprep_kernel,
        grid=(2, b, l // tm),
        in_specs=[pl.BlockSpec((1, tm, w), lambda s, bi, i: (bi, i, s)),
                  pl.BlockSpec((1, 1, w), lambda s, bi, i: (s, 0, 0)),
                  pl.BlockSpec((tm, w), lambda s, bi, i: (i, 0)),
                  pl.BlockSpec((tm, w), lambda s, bi, i: (i, 0)),
                  pl.BlockSpec((w, w), lambda s, bi, i: (0, 0))],
        out_specs=pl.BlockSpec((1, 1, tm, w), lambda s, bi, i: (s, bi, i, 0)),
        out_shape=jax.ShapeDtypeStruct((2, b, l, w), BF16),
        compiler_params=_params(("arbitrary", "arbitrary", "arbitrary")),
        name="qkprep",
    )(p, gains, cos, sin, gmean)


def _attn_kernel(q_ref, k_ref, v_ref, lam_ref, g_ref, o_ref, *, lam_init):
    q = q_ref[0]
    k = k_ref[0]
    v = v_ref[0]
    lane = lax.broadcasted_iota(I32, q.shape, 1)
    zero = jnp.zeros_like(q)

    def softmax_v(qm):
        s = lax.dot_general(qm, k, (((1,), (1,)), ((), ())), preferred_element_type=F32)
        e = jnp.exp(s - jnp.max(s, axis=-1, keepdims=True))
        den = jnp.sum(e, axis=-1, keepdims=True)
        return jnp.dot(e.astype(BF16), v, preferred_element_type=F32) / den

    o1 = softmax_v(jnp.where(lane < HEAD_DIM, q, zero))
    o2 = softmax_v(jnp.where(lane >= HEAD_DIM, q, zero))
    lp = lam_ref[...]
    lam = (jnp.exp(jnp.sum(lp[0:1] * lp[1:2], axis=-1, keepdims=True))
           - jnp.exp(jnp.sum(lp[2:3] * lp[3:4], axis=-1, keepdims=True)) + lam_init)
    o = o1 - lam * o2
    y = o * lax.rsqrt(jnp.mean(o * o, axis=-1, keepdims=True) + EPS) * g_ref[...]
    o_ref[0] = (y * (1.0 - lam_init)).astype(o_ref.dtype)


def diff_attention(q, k, v, lam_params, subln_g, lam_init):
    b, lq, w = q.shape
    lk = k.shape[1]
    hw = w // HEADS
    tq = min(256, lq)
    return pl.pallas_call(
        functools.partial(_attn_kernel, lam_init=lam_init),
        grid=(b, HEADS, lq // tq),
        in_specs=[pl.BlockSpec((1, tq, hw), lambda bi, h, i: (bi, i, h)),
                  pl.BlockSpec((1, lk, hw), lambda bi, h, i: (bi, 0, h)),
                  pl.BlockSpec((1, lk, hw), lambda bi, h, i: (bi, 0, h)),
                  pl.BlockSpec((4, HEAD_DIM), lambda bi, h, i: (0, 0)),
                  pl.BlockSpec((1, hw), lambda bi, h, i: (0, 0))],
        out_specs=pl.BlockSpec((1, tq, hw), lambda bi, h, i: (bi, i, h)),
        out_shape=jax.ShapeDtypeStruct((b, lq, w), BF16),
        compiler_params=_params(("arbitrary", "arbitrary", "arbitrary"), 48 * 1024 * 1024),
        name="diff_attention",
    )(q, k, v, lam_params, subln_g.reshape(1, hw))


def _chunkmlp_kernel(zu_ref, zv_ref, g_ref, ws_ref, bias_ref, o_ref):
    u = _gelu(zu_ref[0].astype(F32))
    gv = _gelu(zv_ref[0].astype(F32))
    mu = jnp.mean(gv, axis=-1, keepdims=True)
    var = jnp.mean(jnp.square(gv - mu), axis=-1, keepdims=True)
    vn = ((gv - mu) * lax.rsqrt(var + EPS) * g_ref[...]).astype(BF16)
    tm = u.shape[0]
    gw = BRANCH_W // CM_GROUPS
    for n in range(tm // CM_CHUNK):
        rows = slice(n * CM_CHUNK, (n + 1) * CM_CHUNK)
        parts = [jnp.dot(ws_ref[g], vn[rows, g * gw:(g + 1) * gw], preferred_element_type=F32)
                 for g in range(CM_GROUPS)]
        s = jnp.concatenate(parts, axis=-1) + bias_ref[...]
        o_ref[0, rows, :] = (u[rows] * s).astype(o_ref.dtype)


def chunk_mlp(p, ln_g, w_s, bias2d):
    b, l, _ = p.shape
    w = BRANCH_W
    tm = min(512, l)
    return pl.pallas_call(
        _chunkmlp_kernel,
        grid=(b, l // tm),
        in_specs=[pl.BlockSpec((1, tm, w), lambda bi, i: (bi, i, 3)),
                  pl.BlockSpec((1, tm, w), lambda bi, i: (bi, i, 4)),
                  pl.BlockSpec((1, w), lambda bi, i: (0, 0)),
                  pl.BlockSpec((CM_GROUPS, CM_CHUNK, CM_CHUNK), lambda bi, i: (0, 0, 0)),
                  pl.BlockSpec((CM_CHUNK, w), lambda bi, i: (0, 0))],
        out_specs=pl.BlockSpec((1, tm, w), lambda bi, i: (bi, i, 0)),
        out_shape=jax.ShapeDtypeStruct((b, l, w), BF16),
        compiler_params=_params(("arbitrary", "arbitrary")),
        name="chunk_mlp",
    )(p, p, ln_g.reshape(1, w), w_s, bias2d)


def _fourier_kernel(z_ref, cl_ref, sl_ref, cc_ref, sc_ref, o_ref, y1_ref, y2_ref, *, scale):
    @pl.when(pl.program_id(1) == 0)
    def _():
        z = z_ref[0]
        y1_ref[...] = jnp.dot(z, cc_ref[...], preferred_element_type=F32).astype(BF16)
        y2_ref[...] = jnp.dot(z, sc_ref[...], preferred_element_type=F32).astype(BF16)

    acc = (jnp.dot(cl_ref[...], y1_ref[...], preferred_element_type=F32)
           - jnp.dot(sl_ref[...], y2_ref[...], preferred_element_type=F32))
    o_ref[0] = (acc * scale).astype(o_ref.dtype)


def fourier_mix(p, cl, sl, cc, sc):
    b, l, _ = p.shape
    w = BRANCH_W
    tm = min(512, l)
    scale = 1.0 / math.sqrt(l * FN_GROUP_W)
    return pl.pallas_call(
        functools.partial(_fourier_kernel, scale=scale),
        grid=(b, l // tm),
        in_specs=[pl.BlockSpec((1, l, w), lambda bi, i: (bi, 0, 5)),
                  pl.BlockSpec((tm, l), lambda bi, i: (i, 0)),
                  pl.BlockSpec((tm, l), lambda bi, i: (i, 0)),
                  pl.BlockSpec((w, w), lambda bi, i: (0, 0)),
                  pl.BlockSpec((w, w), lambda bi, i: (0, 0))],
        out_specs=pl.BlockSpec((1, tm, w), lambda bi, i: (bi, i, 0)),
        out_shape=jax.ShapeDtypeStruct((b, l, w), BF16),
        scratch_shapes=[pltpu.VMEM((l, w), BF16), pltpu.VMEM((l, w), BF16)],
        compiler_params=_params(("arbitrary", "arbitrary"), 48 * 1024 * 1024),
        name="fourier_mix",
    )(p, cl, sl, cc, sc)


def _merge_kernel(a_ref, m_ref, f_ref, gl0_ref, gl1_ref, gl2_ref, bg_ref, wb_ref, wo_ref,
                  x_ref, gate_ref, o_ref):
    s = None
    for n, (br, gl) in enumerate(((a_ref, gl0_ref), (m_ref, gl1_ref), (f_ref, gl2_ref))):
        y = jnp.dot(br[0], wb_ref[n], preferred_element_type=F32)
        t = jax.nn.sigmoid(gl[0].astype(F32) + bg_ref[n]) * y
        s = t if s is None else s + t
    y = jnp.dot(s.astype(BF16), wo_ref[...], preferred_element_type=F32)
    o_ref[0] = x_ref[0] + gate_ref[...] * y


def merge(att, m, f, p, b_gate, w_branch, w_out, x, mod4, *, ctx):
    b, l, d = x.shape
    w = BRANCH_W
    tm = min(256, l)
    row = (lambda bi: 8) if ctx else (lambda bi: bi)
    br_spec = pl.BlockSpec((1, tm, w), lambda bi, i: (bi, i, 0))
    gl_specs = [pl.BlockSpec((1, tm, d), lambda bi, i, n=n: (bi, i, 3 + n)) for n in range(3)]
    return pl.pallas_call(
        _merge_kernel,
        grid=(b, l // tm),
        in_specs=[br_spec, br_spec, br_spec] + gl_specs + [
            pl.BlockSpec((3, 1, d), lambda bi, i: (0, 0, 0)),
            pl.BlockSpec((3, w, d), lambda bi, i: (0, 0, 0)),
            pl.BlockSpec((d, d), lambda bi, i: (0, 0)),
            pl.BlockSpec((1, tm, d), lambda bi, i: (bi, i, 0)),
            pl.BlockSpec((None, None, 1, d), lambda bi, i: (row(bi), 2, 0, 0))],
        out_specs=pl.BlockSpec((1, tm, d), lambda bi, i: (bi, i, 0)),
        out_shape=jax.ShapeDtypeStruct((b, l, d), F32),
        compiler_params=_params(("arbitrary", "arbitrary")),
        name="merge",
    )(att, m, f, p, p, p, b_gate.reshape(3, 1, d), w_branch, w_out, x, mod4)


def _top_rows(s, k, carry=None):
    n = s.shape[0]
    row = lax.broadcasted_iota(I32, s.shape, 0)
    vals, picks = [], []
    for _ in range(k):
        m = jnp.max(s, axis=0, keepdims=True)
        pos = jnp.min(jnp.where(s == m, row, n), axis=0, keepdims=True)
        hit = row == pos
        vals.append(m)
        if carry is None:
            picks.append(pos)
        else:
            picks.append(jnp.sum(jnp.where(hit, carry, 0), axis=0, keepdims=True))
        s = jnp.where(hit, -jnp.inf, s)
    return jnp.concatenate(vals, axis=0), jnp.concatenate(picks, axis=0)


def _route_kernel(hq_ref, keys_ref, idx_ref, g_ref):
    half = PEER_KEYS
    for h in range(PEER_HEADS):
        tops = []
        for part in range(2):
            c = (2 * h + part) * half
            qc = hq_ref[:, c:c + half]
            qn = qc * lax.rsqrt(jnp.mean(qc * qc, axis=-1, keepdims=True) + EPS)
            st = lax.dot_general(keys_ref[2 * h + part], qn.astype(BF16),
                                 (((1,), (1,)), ((), ())), preferred_element_type=F32)
            tops.append(_top_rows(st, PEER_TOPK))
        (s1, i1), (s2, i2) = tops
        cand_s = jnp.concatenate([s1[i:i + 1] + s2 for i in range(PEER_TOPK)], axis=0)
        cand_i = jnp.concatenate([i1[i:i + 1] * PEER_KEYS + i2 for i in range(PEER_TOPK)], axis=0)
        top_s, top_i = _top_rows(cand_s, PEER_TOPK, carry=cand_i)
        e = jnp.exp(top_s - top_s[0:1])
        rows = slice(h * PEER_TOPK, (h + 1) * PEER_TOPK)
        g_ref[rows, :] = e / jnp.sum(e, axis=0, keepdims=True)
        idx_ref[rows, :] = top_i * ROW_WORDS


def peer_route(hq, keys):
    t, n = hq.shape
    tt = min(256, t)
    return pl.pallas_call(
        _route_kernel,
        grid=(t // tt,),
        in_specs=[pl.BlockSpec((tt, n), lambda i: (i, 0)),
                  pl.BlockSpec(keys.shape, lambda i: (0, 0, 0))],
        out_specs=[pl.BlockSpec((PEER_SEL, tt), lambda i: (0, i)),
                   pl.BlockSpec((PEER_SEL, tt), lambda i: (0, i))],
        out_shape=[jax.ShapeDtypeStruct((PEER_SEL, t), I32),
                   jax.ShapeDtypeStruct((PEER_SEL, t), F32)],
        compiler_params=_params(("arbitrary",)),
        name="peer_route",
    )(hq, keys)


def _unpack(slab):
    lo = pltpu.bitcast(slab << 16, F32)
    hi = pltpu.bitcast(slab & HI_MASK, F32)
    return lo, hi


def _peer_u_kernel(idx_ref, h_ref, g_ref, tab_ref, w_ref, tile_ref, acc_ref):
    tt = h_ref.shape[0]
    lane = lax.broadcasted_iota(I32, (PEER_SEL, LANES), 1)

    def token(t, carry):
        hv = h_ref[t]
        ta, tb = hv[0:ROW_WORDS], hv[ROW_WORDS:2 * ROW_WORDS]
        for k in range(PEER_SEL):
            i = pl.multiple_of(idx_ref[k, t], ROW_WORDS)
            lo, hi = _unpack(tab_ref[pl.ds(i, ROW_WORDS), :])
            tile_ref[ROW_WORDS * k:ROW_WORDS * (k + 1), :] = lo * ta + hi * tb
        q = tile_ref[pl.ds(0, PEER_SEL, stride=ROW_WORDS), :]
        for s in range(1, ROW_WORDS):
            q = q + tile_ref[pl.ds(s, PEER_SEL, stride=ROW_WORDS), :]
        col = jnp.sum(q, axis=1, keepdims=True)
        acc_ref[...] = jnp.where(lane == t, col, acc_ref[...])
        return carry

    lax.fori_loop(0, tt, token, 0)
    w_ref[...] = _gelu(acc_ref[...]) * g_ref[...]


def peer_u(idx, h3, g, tab):
    t = h3.shape[0]
    tt = LANES
    return pl.pallas_call(
        _peer_u_kernel,
        grid=(t // tt,),
        in_specs=[pl.BlockSpec((PEER_SEL, tt), lambda i: (0, i), memory_space=pltpu.SMEM),
                  pl.BlockSpec((tt, SUBLANES, LANES), lambda i: (i, 0, 0)),
                  pl.BlockSpec((PEER_SEL, tt), lambda i: (0, i)),
                  pl.BlockSpec(tab.shape, lambda i: (0, 0), pipeline_mode=pl.Buffered(1))],
        out_specs=pl.BlockSpec((PEER_SEL, tt), lambda i: (0, i)),
        out_shape=jax.ShapeDtypeStruct((PEER_SEL, t), F32),
        scratch_shapes=[pltpu.VMEM((PEER_SEL * ROW_WORDS, LANES), F32),
                        pltpu.VMEM((PEER_SEL, LANES), F32)],
        compiler_params=_params(("arbitrary",), 48 * 1024 * 1024),
        name="peer_u",
    )(idx, h3, g, tab)


def _peer_v_kernel(idx_ref, w_ref, x_ref, gate_ref, tab_ref, o_ref):
    tt = x_ref.shape[0]
    gate = gate_ref[...]
    nacc = 2

    def token(t, carry):
        lo_acc = [jnp.zeros((ROW_WORDS, LANES), F32) for _ in range(nacc)]
        hi_acc = [jnp.zeros((ROW_WORDS, LANES), F32) for _ in range(nacc)]
        for k in range(PEER_SEL):
            i = pl.multiple_of(idx_ref[k, t], ROW_WORDS)
            wk = w_ref[k, t]
            lo, hi = _unpack(tab_ref[pl.ds(i, ROW_WORDS), :])
            lo_acc[k % nacc] = lo_acc[k % nacc] + wk * lo
            hi_acc[k % nacc] = hi_acc[k % nacc] + wk * hi
        y = jnp.concatenate([lo_acc[0] + lo_acc[1], hi_acc[0] + hi_acc[1]], axis=0)
        o_ref[t] = x_ref[t] + gate * y
        return carry

    lax.fori_loop(0, tt, token, 0)


def peer_v(idx, w, x3, mod5, tab, *, ctx, seq):
    t = x3.shape[0]
    tt = LANES
    row = (lambda i: 8) if ctx else (lambda i: (i * tt) // seq)
    return pl.pallas_call(
        _peer_v_kernel,
        grid=(t // tt,),
        in_specs=[pl.BlockSpec((PEER_SEL, tt), lambda i: (0, i), memory_space=pltpu.SMEM),
                  pl.BlockSpec((PEER_SEL, tt), lambda i: (0, i), memory_space=pltpu.SMEM),
                  pl.BlockSpec((tt, SUBLANES, LANES), lambda i: (i, 0, 0)),
                  pl.BlockSpec((None, None, SUBLANES, LANES), lambda i: (row(i), 5, 0, 0)),
                  pl.BlockSpec(tab.shape, lambda i: (0, 0), pipeline_mode=pl.Buffered(1))],
        out_specs=pl.BlockSpec((tt, SUBLANES, LANES), lambda i: (i, 0, 0)),
        out_shape=jax.ShapeDtypeStruct(x3.shape, F32),
        compiler_params=_params(("arbitrary",), 48 * 1024 * 1024),
        name="peer_v",
    )(idx, w, x3, mod5, tab)


def _pack_table(tab):
    e, d = tab.shape
    bits = lax.bitcast_convert_type(tab.astype(BF16), jnp.uint16).astype(jnp.uint32)
    word = bits[:, :d // 2] | (bits[:, d // 2:] << 16)
    return lax.bitcast_convert_type(word, I32).reshape(e * ROW_WORDS, LANES)


def _rope_tables(rows):
    r, col = jnp.meshgrid(jnp.arange(rows, dtype=F32), jnp.arange(GRID_W, dtype=F32), indexing="ij")
    freqs = ROPE_THETA ** (-jnp.arange(0, ROPE_AXIS_DIM, 2, dtype=F32) / ROPE_AXIS_DIM)
    ang_r = r.reshape(-1, 1) * freqs
    ang_c = col.reshape(-1, 1) * freqs
    cos = jnp.concatenate([jnp.cos(ang_r)] * 2 + [jnp.cos(ang_c)] * 2, axis=-1)
    sin = jnp.concatenate([-jnp.sin(ang_r), jnp.sin(ang_r), -jnp.sin(ang_c), jnp.sin(ang_c)], axis=-1)
    reps = HEADS * 2
    return jnp.tile(cos, (1, reps)), jnp.tile(sin, (1, reps))


def _dft(n):
    k = jnp.arange(n, dtype=I32)
    ang = ((k[:, None] * k[None, :]) % n).astype(F32) * (2.0 * math.pi / n)
    return jnp.cos(ang), jnp.sin(ang)


def _block_diag(m, groups):
    return jnp.kron(jnp.eye(groups, dtype=m.dtype), m)


def _peer(x, gain, mod4, w_q, keys, u_tab, v_tab, *, ctx):
    b, l, d = x.shape
    t = b * l
    h2, hq = normproj(x, gain, mod4, 3, 4, w_q, ctx=ctx, emit_h=True, out_dtype=F32)
    idx, g = peer_route(hq.reshape(t, -1), keys)
    h3 = h2.reshape(t, SUBLANES, LANES)
    w = peer_u(idx, h3, g, u_tab)
    mod5 = mod4.reshape(mod4.shape[0], 6, SUBLANES, LANES)
    out = peer_v(idx, w, x.reshape(t, SUBLANES, LANES), mod5, v_tab, ctx=ctx, seq=l)
    return out.reshape(b, l, d)


def kernel(x, c, ctx, c_ctx, w_ada, b_ada, norm1_g, norm2_g, w_in, b_gate, q_norm_g, k_norm_g,
           lam_params, subln_g, cm_ln_g, cm_w_s, cm_b_s, w_branch, w_out,
           peer_w_q, peer_sub_keys, peer_u_tab, peer_v_tab):
    bsz, seq, d = x.shape
    lc = ctx.shape[1]
    depth = w_ada.shape[0]
    assert bsz <= 8 and d == SUBLANES * LANES and seq % GRID_W == 0

    cin = jnp.zeros((16, d), F32).at[:bsz].set(c).at[bsz].set(c_ctx)
    mods = ada_mod(cin, w_ada, b_ada)

    cos_l, sin_l = _rope_tables(seq // GRID_W)
    cos_c = jnp.ones((lc, cos_l.shape[1]), F32)
    sin_c = jnp.zeros((lc, cos_l.shape[1]), F32)
    gmean = _block_diag(jnp.full((HEAD_DIM, HEAD_DIM), 1.0 / HEAD_DIM, F32), HEADS * 2).astype(BF16)
    cl_l, sl_l = (m.astype(BF16) for m in _dft(seq))
    cl_c, sl_c = (m.astype(BF16) for m in _dft(lc))
    cg, sg = _dft(FN_GROUP_W)
    cc = _block_diag(cg, FN_GROUPS).astype(BF16)
    sc = _block_diag(sg, FN_GROUPS).astype(BF16)
    qscale = HEAD_DIM ** -0.5

    xc = ctx
    for l in range(depth):
        last = l == depth - 1
        lam_init = 0.8 - 0.6 * math.exp(-0.3 * l)
        mod4 = mods[l].reshape(16, 6, 1, d)
        w_in_l = w_in[l].astype(BF16)
        reps = HEADS * 2
        gains = jnp.stack([jnp.tile(q_norm_g[l], reps) * qscale, jnp.tile(k_norm_g[l], reps)])[:, None, :]
        gains_c = jnp.stack([jnp.tile(q_norm_g[l], reps) * qscale, jnp.tile(k_norm_g[l], reps)])[:, None, :]
        ws = cm_w_s[l].astype(BF16)
        bias2d = jnp.repeat(cm_b_s[l].T, BRANCH_W // CM_GROUPS, axis=1)
        wb = w_branch[l].astype(BF16)
        wo = w_out[l].astype(BF16)
        wq = peer_w_q[l].astype(BF16)
        keys = peer_sub_keys[l].reshape(PEER_HEADS * 2, PEER_KEYS, -1).astype(BF16)
        u_tab = _pack_table(peer_u_tab[l])
        v_tab = _pack_table(peer_v_tab[l])

        (p,) = normproj(x, norm1_g[l], mod4, 0, 1, w_in_l, ctx=False, emit_h=False, out_dtype=BF16)
        (pc,) = normproj(xc, norm1_g[l], mod4, 0, 1, w_in_l, ctx=True, emit_h=False, out_dtype=BF16)
        qk = qkprep(p, gains, cos_l, sin_l, gmean)
        qkc = qkprep(pc, gains_c, cos_c, sin_c, gmean)
        k_all = jnp.concatenate([qkc[1], qk[1]], axis=1)
        v_all = jnp.concatenate([pc[..., 1024:1536], p[..., 1024:1536]], axis=1)
        att = diff_attention(qk[0], k_all, v_all, lam_params[l], subln_g[l], lam_init)
        m = chunk_mlp(p, cm_ln_g[l], ws, bias2d)
        f = fourier_mix(p, cl_l, sl_l, cc, sc)
        x = merge(att, m, f, p, b_gate[l], wb, wo, x, mod4, ctx=False)
        if not last:
            attc = diff_attention(qkc[0], qkc[1], pc[..., 1024:1536], lam_params[l], subln_g[l], lam_init)
            mc = chunk_mlp(pc, cm_ln_g[l], ws, bias2d)
            fc = fourier_mix(pc, cl_c, sl_c, cc, sc)
            xc = merge(attc, mc, fc, pc, b_gate[l], wb, wo, xc, mod4, ctx=True)

        x = _peer(x, norm2_g[l], mod4, wq, keys, u_tab, v_tab, ctx=False)
        if not last:
            xc = _peer(xc, norm2_g[l], mod4, wq, keys, u_tab, v_tab, ctx=True)
    return x
```

```python
import functools
import math

import jax
import jax.numpy as jnp
from jax import lax
from jax.experimental import pallas as pl
from jax.experimental.pallas import tpu as pltpu

F32 = jnp.float32
BF16 = jnp.bfloat16
I32 = jnp.int32

LANES = 128
SUBLANES = 8
VMEM_BYTES = 64 * 1024 * 1024

EPS = 1e-6
GRID_W = 64
HEADS = 4
HEAD_DIM = 64
ROPE_AXIS_DIM = HEAD_DIM // 2
ROPE_THETA = 10000.0
BRANCH_W = 512
CM_CHUNK = 128
CM_GROUPS = 4
FN_GROUPS = 4
FN_GROUP_W = BRANCH_W // FN_GROUPS
PEER_HEADS = 8
PEER_KEYS = 128
PEER_TOPK = 16
PEER_SEL = PEER_HEADS * PEER_TOPK
ROW_WORDS = 4
HI_MASK = -65536


def _vmem_limit(nbytes):
    return int(min(VMEM_BYTES - 4 * 1024 * 1024, max(32 * 1024 * 1024, nbytes)))


def _params(sem, vmem=None):
    return pltpu.CompilerParams(dimension_semantics=sem,
                                vmem_limit_bytes=_vmem_limit(vmem or 0))


def _gelu(x):
    return 0.5 * x * (1.0 + lax.erf(x * (1.0 / math.sqrt(2.0))))


def _ada_kernel(c_ref, w_ref, b_ref, o_ref):
    c = c_ref[...]
    a = c * jax.nn.sigmoid(c)
    o_ref[0] = jnp.dot(a.astype(BF16), w_ref[0].astype(BF16), preferred_element_type=F32) + b_ref[0]


def ada_mod(cin, w_ada, b_ada):
    depth, d, n = w_ada.shape
    r = cin.shape[0]
    tn = 1536
    return pl.pallas_call(
        _ada_kernel,
        grid=(depth, n // tn),
        in_specs=[pl.BlockSpec((r, d), lambda l, j: (0, 0)),
                  pl.BlockSpec((1, d, tn), lambda l, j: (l, 0, j)),
                  pl.BlockSpec((1, 1, tn), lambda l, j: (l, 0, j))],
        out_specs=pl.BlockSpec((1, r, tn), lambda l, j: (l, 0, j)),
        out_shape=jax.ShapeDtypeStruct((depth, r, n), F32),
        compiler_params=_params(("arbitrary", "arbitrary")),
        name="ada_mod",
    )(cin, w_ada, b_ada.reshape(depth, 1, n))


def _normproj_kernel(x_ref, g_ref, sh_ref, sc_ref, w_ref, *refs, emit_h):
    if emit_h:
        h_ref, p_ref, hs_ref = refs
    else:
        p_ref, hs_ref = refs

    @pl.when(pl.program_id(2) == 0)
    def _():
        x = x_ref[0]
        y = x * lax.rsqrt(jnp.mean(x * x, axis=-1, keepdims=True) + EPS) * g_ref[...]
        h = y * (1.0 + sc_ref[...]) + sh_ref[...]
        hs_ref[...] = h.astype(BF16)
        if emit_h:
            h_ref[0] = h

    p_ref[0] = jnp.dot(hs_ref[...], w_ref[...], preferred_element_type=F32).astype(p_ref.dtype)


def normproj(x, gain, mod4, shift_i, scale_i, w, *, ctx, emit_h, out_dtype):
    b, l, d = x.shape
    n = w.shape[1]
    tm = min(512, l)
    tn = 1536 if n % 1536 == 0 else 1024
    assert l % tm == 0 and n % tn == 0
    row = (lambda bi: 8) if ctx else (lambda bi: bi)
    in_specs = [pl.BlockSpec((1, tm, d), lambda bi, i, j: (bi, i, 0)),
                pl.BlockSpec((1, d), lambda bi, i, j: (0, 0)),
                pl.BlockSpec((None, None, 1, d), lambda bi, i, j: (row(bi), shift_i, 0, 0)),
                pl.BlockSpec((None, None, 1, d), lambda bi, i, j: (row(bi), scale_i, 0, 0)),
                pl.BlockSpec((d, tn), lambda bi, i, j: (0, j))]
    out_specs = [pl.BlockSpec((1, tm, tn), lambda bi, i, j: (bi, i, j))]
    out_shape = [jax.ShapeDtypeStruct((b, l, n), out_dtype)]
    if emit_h:
        out_specs = [pl.BlockSpec((1, tm, d), lambda bi, i, j: (bi, i, 0))] + out_specs
        out_shape = [jax.ShapeDtypeStruct((b, l, d), F32)] + out_shape
    return pl.pallas_call(
        functools.partial(_normproj_kernel, emit_h=emit_h),
        grid=(b, l // tm, n // tn),
        in_specs=in_specs, out_specs=out_specs, out_shape=out_shape,
        scratch_shapes=[pltpu.VMEM((tm, d), BF16)],
        compiler_params=_params(("arbitrary", "arbitrary", "arbitrary")),
        name="normproj",
    )(x, gain.reshape(1, d), mod4, mod4, w)


def _qkprep_kernel(p_ref, g_ref, cos_ref, sin_ref, gm_ref, o_ref):
    x = p_ref[0].astype(F32)
    ms = jnp.dot((x * x).astype(BF16), gm_ref[...], preferred_element_type=F32)
    y = x * lax.rsqrt(ms + EPS) * g_ref[0]
    w = y.shape[-1]
    lane = lax.broadcasted_iota(I32, y.shape, 1)
    half = ROPE_AXIS_DIM // 2
    partner = jnp.where((lane % ROPE_AXIS_DIM) < half,
                        pltpu.roll(y, w - half, axis=1), pltpu.roll(y, half, axis=1))
    o_ref[0, 0] = (y * cos_ref[...] + partner * sin_ref[...]).astype(o_ref.dtype)


def qkprep(p, gains, cos, sin, gmean):
    b, l, _ = p.shape
    w = HEADS * 2 * HEAD_DIM
    tm = min(512, l)
    return pl.pallas_call(
        _qkprep_kernel,
        grid=(2, b, l // tm),
        in_specs=[pl.BlockSpec((1, tm, w), lambda s, bi, i: (bi, i, s)),
                  pl.BlockSpec((1, 1, w), lambda s, bi, i: (s, 0, 0)),
                  pl.BlockSpec((tm, w), lambda s, bi, i: (i, 0)),
                  pl.BlockSpec((tm, w), lambda s, bi, i: (i, 0)),
                  pl.BlockSpec((w, w), lambda s, bi, i: (0, 0))],
        out_specs=pl.BlockSpec((1, 1, tm, w), lambda s, bi, i: (s, bi, i, 0)),
        out_shape=jax.ShapeDtypeStruct((2, b, l, w), BF16),
        compiler_params=_params(("arbitrary", "arbitrary", "arbitrary")),
        name="qkprep",
    )(p, gains, cos, sin, gmean)


def _attn_kernel(q_ref, k_ref, v_ref, lam_ref, g_ref, o_ref, *, lam_init):
    q = q_ref[0]
    k = k_ref[0]
    v = v_ref[0]
    lane = lax.broadcasted_iota(I32, q.shape, 1)
    zero = jnp.zeros_like(q)

    def softmax_v(qm):
        s = lax.dot_general(qm, k, (((1,), (1,)), ((), ())), preferred_element_type=F32)
        e = jnp.exp(s - jnp.max(s, axis=-1, keepdims=True))
        den = jnp.sum(e, axis=-1, keepdims=True)
        return jnp.dot(e.astype(BF16), v, preferred_element_type=F32) / den

    o1 = softmax_v(jnp.where(lane < HEAD_DIM, q, zero))
    o2 = softmax_v(jnp.where(lane >= HEAD_DIM, q, zero))
    lp = lam_ref[...]
    lam = (jnp.exp(jnp.sum(lp[0:1] * lp[1:2], axis=-1, keepdims=True))
           - jnp.exp(jnp.sum(lp[2:3] * lp[3:4], axis=-1, keepdims=True)) + lam_init)
    o = o1 - lam * o2
    y = o * lax.rsqrt(jnp.mean(o * o, axis=-1, keepdims=True) + EPS) * g_ref[...]
    o_ref[0] = (y * (1.0 - lam_init)).astype(o_ref.dtype)


def diff_attention(q, k, v, lam_params, subln_g, lam_init):
    b, lq, w = q.shape
    lk = k.shape[1]
    hw = w // HEADS
    tq = min(256, lq)
    return pl.pallas_call(
        functools.partial(_attn_kernel, lam_init=lam_init),
        grid=(b, HEADS, lq // tq),
        in_specs=[pl.BlockSpec((1, tq, hw), lambda bi, h, i: (bi, i, h)),
                  pl.BlockSpec((1, lk, hw), lambda bi, h, i: (bi, 0, h)),
                  pl.BlockSpec((1, lk, hw), lambda bi, h, i: (bi, 0, h)),
                  pl.BlockSpec((4, HEAD_DIM), lambda bi, h, i: (0, 0)),
                  pl.BlockSpec((1, hw), lambda bi, h, i: (0, 0))],
        out_specs=pl.BlockSpec((1, tq, hw), lambda bi, h, i: (bi, i, h)),
        out_shape=jax.ShapeDtypeStruct((b, lq, w), BF16),
        compiler_params=_params(("arbitrary", "arbitrary", "arbitrary"), 48 * 1024 * 1024),
        name="diff_attention",
    )(q, k, v, lam_params, subln_g.reshape(1, hw))


def _chunkmlp_kernel(zu_ref, zv_ref, g_ref, ws_ref, bias_ref, o_ref):
    u = _gelu(zu_ref[0].astype(F32))
    gv = _gelu(zv_ref[0].astype(F32))
    mu = jnp.mean(gv, axis=-1, keepdims=True)
    var = jnp.mean(jnp.square(gv - mu), axis=-1, keepdims=True)
    vn = ((gv - mu) * lax.rsqrt(var + EPS) * g_ref[...]).astype(BF16)
    tm = u.shape[0]
    gw = BRANCH_W // CM_GROUPS
    for n in range(tm // CM_CHUNK):
        rows = slice(n * CM_CHUNK, (n + 1) * CM_CHUNK)
        parts = [jnp.dot(ws_ref[g], vn[rows, g * gw:(g + 1) * gw], preferred_element_type=F32)
                 for g in range(CM_GROUPS)]
        s = jnp.concatenate(parts, axis=-1) + bias_ref[...]
        o_ref[0, rows, :] = (u[rows] * s).astype(o_ref.dtype)


def chunk_mlp(p, ln_g, w_s, bias2d):
    b, l, _ = p.shape
    w = BRANCH_W
    tm = min(512, l)
    return pl.pallas_call(
        _chunkmlp_kernel,
        grid=(b, l // tm),
        in_specs=[pl.BlockSpec((1, tm, w), lambda bi, i: (bi, i, 3)),
                  pl.BlockSpec((1, tm, w), lambda bi, i: (bi, i, 4)),
                  pl.BlockSpec((1, w), lambda bi, i: (0, 0)),
                  pl.BlockSpec((CM_GROUPS, CM_CHUNK, CM_CHUNK), lambda bi, i: (0, 0, 0)),
                  pl.BlockSpec((CM_CHUNK, w), lambda bi, i: (0, 0))],
        out_specs=pl.BlockSpec((1, tm, w), lambda bi, i: (bi, i, 0)),
        out_shape=jax.ShapeDtypeStruct((b, l, w), BF16),
        compiler_params=_params(("arbitrary", "arbitrary")),
        name="chunk_mlp",
    )(p, p, ln_g.reshape(1, w), w_s, bias2d)


def _fourier_kernel(z_ref, cl_ref, sl_ref, cc_ref, sc_ref, o_ref, y1_ref, y2_ref, *, scale):
    @pl.when(pl.program_id(1) == 0)
    def _():
        z = z_ref[0]
        y1_ref[...] = jnp.dot(z, cc_ref[...], preferred_element_type=F32).astype(BF16)
        y2_ref[...] = jnp.dot(z, sc_ref[...], preferred_element_type=F32).astype(BF16)

    acc = (jnp.dot(cl_ref[...], y1_ref[...], preferred_element_type=F32)
           - jnp.dot(sl_ref[...], y2_ref[...], preferred_element_type=F32))
    o_ref[0] = (acc * scale).astype(o_ref.dtype)


def fourier_mix(p, cl, sl, cc, sc):
    b, l, _ = p.shape
    w = BRANCH_W
    tm = min(512, l)
    scale = 1.0 / math.sqrt(l * FN_GROUP_W)
    return pl.pallas_call(
        functools.partial(_fourier_kernel, scale=scale),
        grid=(b, l // tm),
        in_specs=[pl.BlockSpec((1, l, w), lambda bi, i: (bi, 0, 5)),
                  pl.BlockSpec((tm, l), lambda bi, i: (i, 0)),
                  pl.BlockSpec((tm, l), lambda bi, i: (i, 0)),
                  pl.BlockSpec((w, w), lambda bi, i: (0, 0)),
                  pl.BlockSpec((w, w), lambda bi, i: (0, 0))],
        out_specs=pl.BlockSpec((1, tm, w), lambda bi, i: (bi, i, 0)),
        out_shape=jax.ShapeDtypeStruct((b, l, w), BF16),
        scratch_shapes=[pltpu.VMEM((l, w), BF16), pltpu.VMEM((l, w), BF16)],
        compiler_params=_params(("arbitrary", "arbitrary"), 48 * 1024 * 1024),
        name="fourier_mix",
    )(p, cl, sl, cc, sc)


def _merge_kernel(a_ref, m_ref, f_ref, gl0_ref, gl1_ref, gl2_ref, bg_ref, wb_ref, wo_ref,
                  x_ref, gate_ref, o_ref):
    s = None
    for n, (br, gl) in enumerate(((a_ref, gl0_ref), (m_ref, gl1_ref), (f_ref, gl2_ref))):
        y = jnp.dot(br[0], wb_ref[n], preferred_element_type=F32)
        t = jax.nn.sigmoid(gl[0].astype(F32) + bg_ref[n]) * y
        s = t if s is None else s + t
    y = jnp.dot(s.astype(BF16), wo_ref[...], preferred_element_type=F32)
    o_ref[0] = x_ref[0] + gate_ref[...] * y


def merge(att, m, f, p, b_gate, w_branch, w_out, x, mod4, *, ctx):
    b, l, d = x.shape
    w = BRANCH_W
    tm = min(256, l)
    row = (lambda bi: 8) if ctx else (lambda bi: bi)
    br_spec = pl.BlockSpec((1, tm, w), lambda bi, i: (bi, i, 0))
    gl_specs = [pl.BlockSpec((1, tm, d), lambda bi, i, n=n: (bi, i, 3 + n)) for n in range(3)]
    return pl.pallas_call(
        _merge_kernel,
        grid=(b, l // tm),
        in_specs=[br_spec, br_spec, br_spec] + gl_specs + [
            pl.BlockSpec((3, 1, d), lambda bi, i: (0, 0, 0)),
            pl.BlockSpec((3, w, d), lambda bi, i: (0, 0, 0)),
            pl.BlockSpec((d, d), lambda bi, i: (0, 0)),
            pl.BlockSpec((1, tm, d), lambda bi, i: (bi, i, 0)),
            pl.BlockSpec((None, None, 1, d), lambda bi, i: (row(bi), 2, 0, 0))],
        out_specs=pl.BlockSpec((1, tm, d), lambda bi, i: (bi, i, 0)),
        out_shape=jax.ShapeDtypeStruct((b, l, d), F32),
        compiler_params=_params(("arbitrary", "arbitrary")),
        name="merge",
    )(att, m, f, p, p, p, b_gate.reshape(3, 1, d), w_branch, w_out, x, mod4)


def _top_rows(s, k, carry=None):
    n = s.shape[0]
    row = lax.broadcasted_iota(I32, s.shape, 0)
    vals, picks = [], []
    for _ in range(k):
        m = jnp.max(s, axis=0, keepdims=True)
        pos = jnp.min(jnp.where(s == m, row, n), axis=0, keepdims=True)
        hit = row == pos
        vals.append(m)
        if carry is None:
            picks.append(pos)
        else:
            picks.append(jnp.sum(jnp.where(hit, carry, 0), axis=0, keepdims=True))
        s = jnp.where(hit, -jnp.inf, s)
    return jnp.concatenate(vals, axis=0), jnp.concatenate(picks, axis=0)


def _candidates(s1, i1, s2, i2):
    k = s1.shape[0]
    ps, pi = [], []
    i = 0
    while k // (i + 1) > 1:
        n = k // (i + 1)
        rows = -(-n // SUBLANES) * SUBLANES
        v = s1[i:i + 1] + s2[0:rows]
        if n < rows:
            v = jnp.where(lax.broadcasted_iota(I32, v.shape, 0) < n, v, -jnp.inf)
        ps.append(v)
        pi.append(i1[i:i + 1] * PEER_KEYS + i2[0:rows])
        i += 1
    ps.append(s1[i:] + s2[0:1])
    pi.append(i1[i:] * PEER_KEYS + i2[0:1])
    return jnp.concatenate(ps, axis=0), jnp.concatenate(pi, axis=0)


def _route_kernel(hq_ref, keys_ref, idx_ref, g_ref):
    half = PEER_KEYS
    picks = []
    for h in range(PEER_HEADS):
        tops = []
        for part in range(2):
            c = (2 * h + part) * half
            qc = hq_ref[:, c:c + half]
            qn = qc * lax.rsqrt(jnp.mean(qc * qc, axis=-1, keepdims=True) + EPS)
            st = lax.dot_general(keys_ref[2 * h + part], qn.astype(BF16),
                                 (((1,), (1,)), ((), ())), preferred_element_type=F32)
            tops.append(_top_rows(st, PEER_TOPK))
        cand_s, cand_i = _candidates(*tops[0], *tops[1])
        top_s, top_i = _top_rows(cand_s, PEER_TOPK, carry=cand_i)
        e = jnp.exp(top_s - top_s[0:1])
        rows = slice(h * PEER_TOPK, (h + 1) * PEER_TOPK)
        g_ref[rows, :] = e / jnp.sum(e, axis=0, keepdims=True)
        picks.append(top_i * ROW_WORDS)
    idx_ref[...] = jnp.concatenate(picks, axis=0).T


def peer_route(hq, keys):
    t, n = hq.shape
    tt = min(256, t)
    return pl.pallas_call(
        _route_kernel,
        grid=(t // tt,),
        in_specs=[pl.BlockSpec((tt, n), lambda i: (i, 0)),
                  pl.BlockSpec(keys.shape, lambda i: (0, 0, 0))],
        out_specs=[pl.BlockSpec((tt, PEER_SEL), lambda i: (i, 0)),
                   pl.BlockSpec((PEER_SEL, tt), lambda i: (0, i))],
        out_shape=[jax.ShapeDtypeStruct((t, PEER_SEL), I32),
                   jax.ShapeDtypeStruct((PEER_SEL, t), F32)],
        compiler_params=_params(("arbitrary",)),
        name="peer_route",
    )(hq, keys)


def _unpack(slab):
    lo = pltpu.bitcast(slab << 16, F32)
    hi = pltpu.bitcast(slab & HI_MASK, F32)
    return lo, hi


def _peer_u_kernel(idx_ref, h_ref, g_ref, tab_ref, w_ref, tile_a, tile_b, acc_ref):
    tt = h_ref.shape[0]
    lane = lax.broadcasted_iota(I32, (PEER_SEL, LANES), 1)

    def products(t, tile_ref):
        hv = h_ref[t]
        ta, tb = hv[0:ROW_WORDS], hv[ROW_WORDS:2 * ROW_WORDS]
        for k in range(PEER_SEL):
            i = pl.multiple_of(idx_ref[t, k], ROW_WORDS)
            lo, hi = _unpack(tab_ref[pl.ds(i, ROW_WORDS), :])
            tile_ref[ROW_WORDS * k:ROW_WORDS * (k + 1), :] = lo * ta + hi * tb

    def reduce(t, tile_ref):
        q = tile_ref[pl.ds(0, PEER_SEL, stride=ROW_WORDS), :]
        for s in range(1, ROW_WORDS):
            q = q + tile_ref[pl.ds(s, PEER_SEL, stride=ROW_WORDS), :]
        col = jnp.sum(q, axis=1, keepdims=True)
        acc_ref[...] = jnp.where(lane == t, col, acc_ref[...])

    tile_b[...] = jnp.zeros_like(tile_b)
    acc_ref[...] = jnp.zeros_like(acc_ref)

    def pair(j, carry):
        t = 2 * j
        products(t, tile_a)
        reduce(t - 1, tile_b)
        products(t + 1, tile_b)
        reduce(t, tile_a)
        return carry

    lax.fori_loop(0, tt // 2, pair, 0)
    reduce(tt - 1, tile_b)
    w_ref[...] = (_gelu(acc_ref[...]) * g_ref[...]).T


def peer_u(idx, h3, g, tab):
    t = h3.shape[0]
    tt = LANES
    return pl.pallas_call(
        _peer_u_kernel,
        grid=(t // tt,),
        in_specs=[pl.BlockSpec((tt, PEER_SEL), lambda i: (i, 0), memory_space=pltpu.SMEM),
                  pl.BlockSpec((tt, SUBLANES, LANES), lambda i: (i, 0, 0)),
                  pl.BlockSpec((PEER_SEL, tt), lambda i: (0, i)),
                  pl.BlockSpec(tab.shape, lambda i: (0, 0), pipeline_mode=pl.Buffered(1))],
        out_specs=pl.BlockSpec((tt, PEER_SEL), lambda i: (i, 0)),
        out_shape=jax.ShapeDtypeStruct((t, PEER_SEL), F32),
        scratch_shapes=[pltpu.VMEM((PEER_SEL * ROW_WORDS, LANES), F32),
                        pltpu.VMEM((PEER_SEL * ROW_WORDS, LANES), F32),
                        pltpu.VMEM((PEER_SEL, LANES), F32)],
        compiler_params=_params(("arbitrary",), 48 * 1024 * 1024),
        name="peer_u",
    )(idx, h3, g, tab)


def _peer_v_kernel(idx_ref, w_ref, x_ref, gate_ref, tab_ref, o_ref):
    tt = x_ref.shape[0]
    gate = gate_ref[...]
    nacc = 2

    def token(t, carry):
        lo_acc = [jnp.zeros((ROW_WORDS, LANES), F32) for _ in range(nacc)]
        hi_acc = [jnp.zeros((ROW_WORDS, LANES), F32) for _ in range(nacc)]
        for k in range(PEER_SEL):
            i = pl.multiple_of(idx_ref[t, k], ROW_WORDS)
            wk = w_ref[t, k]
            lo, hi = _unpack(tab_ref[pl.ds(i, ROW_WORDS), :])
            lo_acc[k % nacc] = lo_acc[k % nacc] + wk * lo
            hi_acc[k % nacc] = hi_acc[k % nacc] + wk * hi
        y = jnp.concatenate([lo_acc[0] + lo_acc[1], hi_acc[0] + hi_acc[1]], axis=0)
        o_ref[t] = x_ref[t] + gate * y
        return carry

    lax.fori_loop(0, tt, token, 0)


def peer_v(idx, w, x3, mod5, tab, *, ctx, seq):
    t = x3.shape[0]
    tt = LANES
    row = (lambda i: 8) if ctx else (lambda i: (i * tt) // seq)
    return pl.pallas_call(
        _peer_v_kernel,
        grid=(t // tt,),
        in_specs=[pl.BlockSpec((tt, PEER_SEL), lambda i: (i, 0), memory_space=pltpu.SMEM),
                  pl.BlockSpec((tt, PEER_SEL), lambda i: (i, 0), memory_space=pltpu.SMEM),
                  pl.BlockSpec((tt, SUBLANES, LANES), lambda i: (i, 0, 0)),
                  pl.BlockSpec((None, None, SUBLANES, LANES), lambda i: (row(i), 5, 0, 0)),
                  pl.BlockSpec(tab.shape, lambda i: (0, 0), pipeline_mode=pl.Buffered(1))],
        out_specs=pl.BlockSpec((tt, SUBLANES, LANES), lambda i: (i, 0, 0)),
        out_shape=jax.ShapeDtypeStruct(x3.shape, F32),
        compiler_params=_params(("arbitrary",), 48 * 1024 * 1024),
        name="peer_v",
    )(idx, w, x3, mod5, tab)


def _pack_table(tab):
    e, d = tab.shape
    bits = lax.bitcast_convert_type(tab.astype(BF16), jnp.uint16).astype(jnp.uint32)
    word = bits[:, :d // 2] | (bits[:, d // 2:] << 16)
    return lax.bitcast_convert_type(word, I32).reshape(e * ROW_WORDS, LANES)


def _rope_tables(rows):
    r, col = jnp.meshgrid(jnp.arange(rows, dtype=F32), jnp.arange(GRID_W, dtype=F32), indexing="ij")
    freqs = ROPE_THETA ** (-jnp.arange(0, ROPE_AXIS_DIM, 2, dtype=F32) / ROPE_AXIS_DIM)
    ang_r = r.reshape(-1, 1) * freqs
    ang_c = col.reshape(-1, 1) * freqs
    cos = jnp.concatenate([jnp.cos(ang_r)] * 2 + [jnp.cos(ang_c)] * 2, axis=-1)
    sin = jnp.concatenate([-jnp.sin(ang_r), jnp.sin(ang_r), -jnp.sin(ang_c), jnp.sin(ang_c)], axis=-1)
    reps = HEADS * 2
    return jnp.tile(cos, (1, reps)), jnp.tile(sin, (1, reps))


def _dft(n):
    k = jnp.arange(n, dtype=I32)
    ang = ((k[:, None] * k[None, :]) % n).astype(F32) * (2.0 * math.pi / n)
    return jnp.cos(ang), jnp.sin(ang)


def _block_diag(m, groups):
    return jnp.kron(jnp.eye(groups, dtype=m.dtype), m)


def _peer(x, gain, mod4, w_q, keys, u_tab, v_tab, *, ctx):
    b, l, d = x.shape
    t = b * l
    h2, hq = normproj(x, gain, mod4, 3, 4, w_q, ctx=ctx, emit_h=True, out_dtype=F32)
    idx, g = peer_route(hq.reshape(t, -1), keys)
    h3 = h2.reshape(t, SUBLANES, LANES)
    w = peer_u(idx, h3, g, u_tab)
    mod5 = mod4.reshape(mod4.shape[0], 6, SUBLANES, LANES)
    out = peer_v(idx, w, x.reshape(t, SUBLANES, LANES), mod5, v_tab, ctx=ctx, seq=l)
    return out.reshape(b, l, d)


def kernel(x, c, ctx, c_ctx, w_ada, b_ada, norm1_g, norm2_g, w_in, b_gate, q_norm_g, k_norm_g,
           lam_params, subln_g, cm_ln_g, cm_w_s, cm_b_s, w_branch, w_out,
           peer_w_q, peer_sub_keys, peer_u_tab, peer_v_tab):
    bsz, seq, d = x.shape
    lc = ctx.shape[1]
    depth = w_ada.shape[0]
    assert bsz <= 8 and d == SUBLANES * LANES and seq % GRID_W == 0

    cin = jnp.zeros((16, d), F32).at[:bsz].set(c).at[bsz].set(c_ctx)
    mods = ada_mod(cin, w_ada, b_ada)

    cos_l, sin_l = _rope_tables(seq // GRID_W)
    cos_c = jnp.ones((lc, cos_l.shape[1]), F32)
    sin_c = jnp.zeros((lc, cos_l.shape[1]), F32)
    gmean = _block_diag(jnp.full((HEAD_DIM, HEAD_DIM), 1.0 / HEAD_DIM, F32), HEADS * 2).astype(BF16)
    cl_l, sl_l = (m.astype(BF16) for m in _dft(seq))
    cl_c, sl_c = (m.astype(BF16) for m in _dft(lc))
    cg, sg = _dft(FN_GROUP_W)
    cc = _block_diag(cg, FN_GROUPS).astype(BF16)
    sc = _block_diag(sg, FN_GROUPS).astype(BF16)
    qscale = HEAD_DIM ** -0.5

    xc = ctx
    for l in range(depth):
        last = l == depth - 1
        lam_init = 0.8 - 0.6 * math.exp(-0.3 * l)
        mod4 = mods[l].reshape(16, 6, 1, d)
        w_in_l = w_in[l].astype(BF16)
        reps = HEADS * 2
        gains = jnp.stack([jnp.tile(q_norm_g[l], reps) * qscale, jnp.tile(k_norm_g[l], reps)])[:, None, :]
        gains_c = jnp.stack([jnp.tile(q_norm_g[l], reps) * qscale, jnp.tile(k_norm_g[l], reps)])[:, None, :]
        ws = cm_w_s[l].astype(BF16)
        bias2d = jnp.repeat(cm_b_s[l].T, BRANCH_W // CM_GROUPS, axis=1)
        wb = w_branch[l].astype(BF16)
        wo = w_out[l].astype(BF16)
        wq = peer_w_q[l].astype(BF16)
        keys = peer_sub_keys[l].reshape(PEER_HEADS * 2, PEER_KEYS, -1).astype(BF16)
        u_tab = _pack_table(peer_u_tab[l])
        v_tab = _pack_table(peer_v_tab[l])

        (p,) = normproj(x, norm1_g[l], mod4, 0, 1, w_in_l, ctx=False, emit_h=False, out_dtype=BF16)
        (pc,) = normproj(xc, norm1_g[l], mod4, 0, 1, w_in_l, ctx=True, emit_h=False, out_dtype=BF16)
        qk = qkprep(p, gains, cos_l, sin_l, gmean)
        qkc = qkprep(pc, gains_c, cos_c, sin_c, gmean)
        k_all = jnp.concatenate([qkc[1], qk[1]], axis=1)
        v_all = jnp.concatenate([pc[..., 1024:1536], p[..., 1024:1536]], axis=1)
        att = diff_attention(qk[0], k_all, v_all, lam_params[l], subln_g[l], lam_init)
        m = chunk_mlp(p, cm_ln_g[l], ws, bias2d)
        f = fourier_mix(p, cl_l, sl_l, cc, sc)
        x = merge(att, m, f, p, b_gate[l], wb, wo, x, mod4, ctx=False)
        if not last:
            attc = diff_attention(qkc[0], qkc[1], pc[..., 1024:1536], lam_params[l], subln_g[l], lam_init)
            mc = chunk_mlp(pc, cm_ln_g[l], ws, bias2d)
            fc = fourier_mix(pc, cl_c, sl_c, cc, sc)
            xc = merge(attc, mc, fc, pc, b_gate[l], wb, wo, xc, mod4, ctx=True)

        x = _peer(x, norm2_g[l], mod4, wq, keys, u_tab, v_tab, ctx=False)
        if not last:
            xc = _peer(xc, norm2_g[l], mod4, wq, keys, u_tab, v_tab, ctx=True)
    return x
```

```python
import functools
import math

import jax
import jax.numpy as jnp
from jax import lax
from jax.experimental import pallas as pl
from jax.experimental.pallas import tpu as pltpu

F32 = jnp.float32
BF16 = jnp.bfloat16
I32 = jnp.int32

LANES = 128
SUBLANES = 8
VMEM_BYTES = 64 * 1024 * 1024

EPS = 1e-6
GRID_W = 64
HEADS = 4
HEAD_DIM = 64
ROPE_AXIS_DIM = HEAD_DIM // 2
ROPE_THETA = 10000.0
BRANCH_W = 512
CM_CHUNK = 128
CM_GROUPS = 4
FN_GROUPS = 4
FN_GROUP_W = BRANCH_W // FN_GROUPS
PEER_HEADS = 8
PEER_KEYS = 128
PEER_TOPK = 16
PEER_SEL = PEER_HEADS * PEER_TOPK
ROW_WORDS = 4
HI_MASK = -65536
PEER_TOKENS = 32
PEER_U_SLOTS = 4
PEER_V_SLOTS = 4


def _vmem_limit(nbytes):
    return int(min(VMEM_BYTES - 4 * 1024 * 1024, max(32 * 1024 * 1024, nbytes)))


def _params(sem, vmem=None):
    return pltpu.CompilerParams(dimension_semantics=sem,
                                vmem_limit_bytes=_vmem_limit(vmem or 0))


def _gelu(x):
    return 0.5 * x * (1.0 + lax.erf(x * (1.0 / math.sqrt(2.0))))


def _ada_kernel(c_ref, w_ref, b_ref, o_ref):
    c = c_ref[...]
    a = c * jax.nn.sigmoid(c)
    o_ref[0] = jnp.dot(a.astype(BF16), w_ref[0].astype(BF16), preferred_element_type=F32) + b_ref[0]


def ada_mod(cin, w_ada, b_ada):
    depth, d, n = w_ada.shape
    r = cin.shape[0]
    tn = 1536
    return pl.pallas_call(
        _ada_kernel,
        grid=(depth, n // tn),
        in_specs=[pl.BlockSpec((r, d), lambda l, j: (0, 0)),
                  pl.BlockSpec((1, d, tn), lambda l, j: (l, 0, j)),
                  pl.BlockSpec((1, 1, tn), lambda l, j: (l, 0, j))],
        out_specs=pl.BlockSpec((1, r, tn), lambda l, j: (l, 0, j)),
        out_shape=jax.ShapeDtypeStruct((depth, r, n), F32),
        compiler_params=_params(("arbitrary", "arbitrary")),
        name="ada_mod",
    )(cin, w_ada, b_ada.reshape(depth, 1, n))


def _normproj_kernel(x_ref, g_ref, sh_ref, sc_ref, w_ref, *refs, emit_h):
    if emit_h:
        h_ref, p_ref, hs_ref = refs
    else:
        p_ref, hs_ref = refs

    @pl.when(pl.program_id(2) == 0)
    def _():
        x = x_ref[0]
        y = x * lax.rsqrt(jnp.mean(x * x, axis=-1, keepdims=True) + EPS) * g_ref[...]
        h = y * (1.0 + sc_ref[...]) + sh_ref[...]
        hs_ref[...] = h.astype(BF16)
        if emit_h:
            h_ref[0] = h

    p_ref[0] = jnp.dot(hs_ref[...], w_ref[...], preferred_element_type=F32).astype(p_ref.dtype)


def normproj(x, gain, mod4, shift_i, scale_i, w, *, ctx, emit_h, out_dtype):
    b, l, d = x.shape
    n = w.shape[1]
    tm = min(512, l)
    tn = 1536 if n % 1536 == 0 else 1024
    assert l % tm == 0 and n % tn == 0
    row = (lambda bi: 8) if ctx else (lambda bi: bi)
    in_specs = [pl.BlockSpec((1, tm, d), lambda bi, i, j: (bi, i, 0)),
                pl.BlockSpec((1, d), lambda bi, i, j: (0, 0)),
                pl.BlockSpec((None, None, 1, d), lambda bi, i, j: (row(bi), shift_i, 0, 0)),
                pl.BlockSpec((None, None, 1, d), lambda bi, i, j: (row(bi), scale_i, 0, 0)),
                pl.BlockSpec((d, tn), lambda bi, i, j: (0, j))]
    out_specs = [pl.BlockSpec((1, tm, tn), lambda bi, i, j: (bi, i, j))]
    out_shape = [jax.ShapeDtypeStruct((b, l, n), out_dtype)]
    if emit_h:
        out_specs = [pl.BlockSpec((1, tm, d), lambda bi, i, j: (bi, i, 0))] + out_specs
        out_shape = [jax.ShapeDtypeStruct((b, l, d), F32)] + out_shape
    return pl.pallas_call(
        functools.partial(_normproj_kernel, emit_h=emit_h),
        grid=(b, l // tm, n // tn),
        in_specs=in_specs, out_specs=out_specs, out_shape=out_shape,
        scratch_shapes=[pltpu.VMEM((tm, d), BF16)],
        compiler_params=_params(("arbitrary", "arbitrary", "arbitrary")),
        name="normproj",
    )(x, gain.reshape(1, d), mod4, mod4, w)


def _qkprep_kernel(p_ref, g_ref, cos_ref, sin_ref, gm_ref, o_ref):
    x = p_ref[0].astype(F32)
    ms = jnp.dot((x * x).astype(BF16), gm_ref[...], preferred_element_type=F32)
    y = x * lax.rsqrt(ms + EPS) * g_ref[0]
    w = y.shape[-1]
    lane = lax.broadcasted_iota(I32, y.shape, 1)
    half = ROPE_AXIS_DIM // 2
    partner = jnp.where((lane % ROPE_AXIS_DIM) < half,
                        pltpu.roll(y, w - half, axis=1), pltpu.roll(y, half, axis=1))
    o_ref[0, 0] = (y * cos_ref[...] + partner * sin_ref[...]).astype(o_ref.dtype)


def qkprep(p, gains, cos, sin, gmean):
    b, l, _ = p.shape
    w = HEADS * 2 * HEAD_DIM
    tm = min(512, l)
    return pl.pallas_call(
        _qkprep_kernel,
        grid=(2, b, l // tm),
        in_specs=[pl.BlockSpec((1, tm, w), lambda s, bi, i: (bi, i, s)),
                  pl.BlockSpec((1, 1, w), lambda s, bi, i: (s, 0, 0)),
                  pl.BlockSpec((tm, w), lambda s, bi, i: (i, 0)),
                  pl.BlockSpec((tm, w), lambda s, bi, i: (i, 0)),
                  pl.BlockSpec((w, w), lambda s, bi, i: (0, 0))],
        out_specs=pl.BlockSpec((1, 1, tm, w), lambda s, bi, i: (s, bi, i, 0)),
        out_shape=jax.ShapeDtypeStruct((2, b, l, w), BF16),
        compiler_params=_params(("arbitrary", "arbitrary", "arbitrary")),
        name="qkprep",
    )(p, gains, cos, sin, gmean)


def _attn_kernel(q_ref, k_ref, v_ref, lam_ref, g_ref, o_ref, *, lam_init):
    q = q_ref[0]
    k = k_ref[0]
    v = v_ref[0]
    lane = lax.broadcasted_iota(I32, q.shape, 1)
    zero = jnp.zeros_like(q)

    def softmax_v(qm):
        s = lax.dot_general(qm, k, (((1,), (1,)), ((), ())), preferred_element_type=F32)
        e = jnp.exp(s - jnp.max(s, axis=-1, keepdims=True))
        den = jnp.sum(e, axis=-1, keepdims=True)
        return jnp.dot(e.astype(BF16), v, preferred_element_type=F32) / den

    o1 = softmax_v(jnp.where(lane < HEAD_DIM, q, zero))
    o2 = softmax_v(jnp.where(lane >= HEAD_DIM, q, zero))
    lp = lam_ref[...]
    lam = (jnp.exp(jnp.sum(lp[0:1] * lp[1:2], axis=-1, keepdims=True))
           - jnp.exp(jnp.sum(lp[2:3] * lp[3:4], axis=-1, keepdims=True)) + lam_init)
    o = o1 - lam * o2
    y = o * lax.rsqrt(jnp.mean(o * o, axis=-1, keepdims=True) + EPS) * g_ref[...]
    o_ref[0] = (y * (1.0 - lam_init)).astype(o_ref.dtype)


def diff_attention(q, k, v, lam_params, subln_g, lam_init):
    b, lq, w = q.shape
    lk = k.shape[1]
    hw = w // HEADS
    tq = min(256, lq)
    return pl.pallas_call(
        functools.partial(_attn_kernel, lam_init=lam_init),
        grid=(b, HEADS, lq // tq),
        in_specs=[pl.BlockSpec((1, tq, hw), lambda bi, h, i: (bi, i, h)),
                  pl.BlockSpec((1, lk, hw), lambda bi, h, i: (bi, 0, h)),
                  pl.BlockSpec((1, lk, hw), lambda bi, h, i: (bi, 0, h)),
                  pl.BlockSpec((4, HEAD_DIM), lambda bi, h, i: (0, 0)),
                  pl.BlockSpec((1, hw), lambda bi, h, i: (0, 0))],
        out_specs=pl.BlockSpec((1, tq, hw), lambda bi, h, i: (bi, i, h)),
        out_shape=jax.ShapeDtypeStruct((b, lq, w), BF16),
        compiler_params=_params(("arbitrary", "arbitrary", "arbitrary"), 48 * 1024 * 1024),
        name="diff_attention",
    )(q, k, v, lam_params, subln_g.reshape(1, hw))


def _chunkmlp_kernel(zu_ref, zv_ref, g_ref, ws_ref, bias_ref, o_ref):
    u = _gelu(zu_ref[0].astype(F32))
    gv = _gelu(zv_ref[0].astype(F32))
    mu = jnp.mean(gv, axis=-1, keepdims=True)
    var = jnp.mean(jnp.square(gv - mu), axis=-1, keepdims=True)
    vn = ((gv - mu) * lax.rsqrt(var + EPS) * g_ref[...]).astype(BF16)
    tm = u.shape[0]
    gw = BRANCH_W // CM_GROUPS
    for n in range(tm // CM_CHUNK):
        rows = slice(n * CM_CHUNK, (n + 1) * CM_CHUNK)
        parts = [jnp.dot(ws_ref[g], vn[rows, g * gw:(g + 1) * gw], preferred_element_type=F32)
                 for g in range(CM_GROUPS)]
        s = jnp.concatenate(parts, axis=-1) + bias_ref[...]
        o_ref[0, rows, :] = (u[rows] * s).astype(o_ref.dtype)


def chunk_mlp(p, ln_g, w_s, bias2d):
    b, l, _ = p.shape
    w = BRANCH_W
    tm = min(512, l)
    return pl.pallas_call(
        _chunkmlp_kernel,
        grid=(b, l // tm),
        in_specs=[pl.BlockSpec((1, tm, w), lambda bi, i: (bi, i, 3)),
                  pl.BlockSpec((1, tm, w), lambda bi, i: (bi, i, 4)),
                  pl.BlockSpec((1, w), lambda bi, i: (0, 0)),
                  pl.BlockSpec((CM_GROUPS, CM_CHUNK, CM_CHUNK), lambda bi, i: (0, 0, 0)),
                  pl.BlockSpec((CM_CHUNK, w), lambda bi, i: (0, 0))],
        out_specs=pl.BlockSpec((1, tm, w), lambda bi, i: (bi, i, 0)),
        out_shape=jax.ShapeDtypeStruct((b, l, w), BF16),
        compiler_params=_params(("arbitrary", "arbitrary")),
        name="chunk_mlp",
    )(p, p, ln_g.reshape(1, w), w_s, bias2d)


def _fourier_kernel(z_ref, cl_ref, sl_ref, cc_ref, sc_ref, o_ref, y1_ref, y2_ref, *, scale):
    @pl.when(pl.program_id(1) == 0)
    def _():
        z = z_ref[0]
        y1_ref[...] = jnp.dot(z, cc_ref[...], preferred_element_type=F32).astype(BF16)
        y2_ref[...] = jnp.dot(z, sc_ref[...], preferred_element_type=F32).astype(BF16)

    acc = (jnp.dot(cl_ref[...], y1_ref[...], preferred_element_type=F32)
           - jnp.dot(sl_ref[...], y2_ref[...], preferred_element_type=F32))
    o_ref[0] = (acc * scale).astype(o_ref.dtype)


def fourier_mix(p, cl, sl, cc, sc):
    b, l, _ = p.shape
    w = BRANCH_W
    tm = min(512, l)
    scale = 1.0 / math.sqrt(l * FN_GROUP_W)
    return pl.pallas_call(
        functools.partial(_fourier_kernel, scale=scale),
        grid=(b, l // tm),
        in_specs=[pl.BlockSpec((1, l, w), lambda bi, i: (bi, 0, 5)),
                  pl.BlockSpec((tm, l), lambda bi, i: (i, 0)),
                  pl.BlockSpec((tm, l), lambda bi, i: (i, 0)),
                  pl.BlockSpec((w, w), lambda bi, i: (0, 0)),
                  pl.BlockSpec((w, w), lambda bi, i: (0, 0))],
        out_specs=pl.BlockSpec((1, tm, w), lambda bi, i: (bi, i, 0)),
        out_shape=jax.ShapeDtypeStruct((b, l, w), BF16),
        scratch_shapes=[pltpu.VMEM((l, w), BF16), pltpu.VMEM((l, w), BF16)],
        compiler_params=_params(("arbitrary", "arbitrary"), 48 * 1024 * 1024),
        name="fourier_mix",
    )(p, cl, sl, cc, sc)


def _merge_kernel(a_ref, m_ref, f_ref, gl0_ref, gl1_ref, gl2_ref, bg_ref, wb_ref, wo_ref,
                  x_ref, gate_ref, o_ref):
    s = None
    for n, (br, gl) in enumerate(((a_ref, gl0_ref), (m_ref, gl1_ref), (f_ref, gl2_ref))):
        y = jnp.dot(br[0], wb_ref[n], preferred_element_type=F32)
        t = jax.nn.sigmoid(gl[0].astype(F32) + bg_ref[n]) * y
        s = t if s is None else s + t
    y = jnp.dot(s.astype(BF16), wo_ref[...], preferred_element_type=F32)
    o_ref[0] = x_ref[0] + gate_ref[...] * y


def merge(att, m, f, p, b_gate, w_branch, w_out, x, mod4, *, ctx):
    b, l, d = x.shape
    w = BRANCH_W
    tm = min(256, l)
    row = (lambda bi: 8) if ctx else (lambda bi: bi)
    br_spec = pl.BlockSpec((1, tm, w), lambda bi, i: (bi, i, 0))
    gl_specs = [pl.BlockSpec((1, tm, d), lambda bi, i, n=n: (bi, i, 3 + n)) for n in range(3)]
    return pl.pallas_call(
        _merge_kernel,
        grid=(b, l // tm),
        in_specs=[br_spec, br_spec, br_spec] + gl_specs + [
            pl.BlockSpec((3, 1, d), lambda bi, i: (0, 0, 0)),
            pl.BlockSpec((3, w, d), lambda bi, i: (0, 0, 0)),
            pl.BlockSpec((d, d), lambda bi, i: (0, 0)),
            pl.BlockSpec((1, tm, d), lambda bi, i: (bi, i, 0)),
            pl.BlockSpec((None, None, 1, d), lambda bi, i: (row(bi), 2, 0, 0))],
        out_specs=pl.BlockSpec((1, tm, d), lambda bi, i: (bi, i, 0)),
        out_shape=jax.ShapeDtypeStruct((b, l, d), F32),
        compiler_params=_params(("arbitrary", "arbitrary")),
        name="merge",
    )(att, m, f, p, p, p, b_gate.reshape(3, 1, d), w_branch, w_out, x, mod4)


def _top_rows(s, k, carry=None):
    n = s.shape[0]
    row = lax.broadcasted_iota(I32, s.shape, 0)
    vals, picks = [], []
    for _ in range(k):
        m = jnp.max(s, axis=0, keepdims=True)
        pos = jnp.min(jnp.where(s == m, row, n), axis=0, keepdims=True)
        hit = row == pos
        vals.append(m)
        if carry is None:
            picks.append(pos)
        else:
            picks.append(jnp.sum(jnp.where(hit, carry, 0), axis=0, keepdims=True))
        s = jnp.where(hit, -jnp.inf, s)
    return jnp.concatenate(vals, axis=0), jnp.concatenate(picks, axis=0)


def _candidates(s1, i1, s2, i2):
    k = s1.shape[0]
    ps, pi = [], []
    i = 0
    while k // (i + 1) > 1:
        n = k // (i + 1)
        rows = -(-n // SUBLANES) * SUBLANES
        v = s1[i:i + 1] + s2[0:rows]
        if n < rows:
            v = jnp.where(lax.broadcasted_iota(I32, v.shape, 0) < n, v, -jnp.inf)
        ps.append(v)
        pi.append(i1[i:i + 1] * PEER_KEYS + i2[0:rows])
        i += 1
    ps.append(s1[i:] + s2[0:1])
    pi.append(i1[i:] * PEER_KEYS + i2[0:1])
    return jnp.concatenate(ps, axis=0), jnp.concatenate(pi, axis=0)


def _route_kernel(hq_ref, keys_ref, idx_ref, g_ref):
    half = PEER_KEYS
    picks, gates = [], []
    for h in range(PEER_HEADS):
        tops = []
        for part in range(2):
            c = (2 * h + part) * half
            qc = hq_ref[:, c:c + half]
            qn = qc * lax.rsqrt(jnp.mean(qc * qc, axis=-1, keepdims=True) + EPS)
            st = lax.dot_general(keys_ref[2 * h + part], qn.astype(BF16),
                                 (((1,), (1,)), ((), ())), preferred_element_type=F32)
            tops.append(_top_rows(st, PEER_TOPK))
        cand_s, cand_i = _candidates(*tops[0], *tops[1])
        top_s, top_i = _top_rows(cand_s, PEER_TOPK, carry=cand_i)
        e = jnp.exp(top_s - top_s[0:1])
        gates.append(e / jnp.sum(e, axis=0, keepdims=True))
        picks.append(top_i * ROW_WORDS)
    idx_ref[...] = jnp.concatenate(picks, axis=0).T
    g_ref[...] = jnp.concatenate(gates, axis=0).T


def peer_route(hq, keys):
    t, n = hq.shape
    tt = min(256, t)
    return pl.pallas_call(
        _route_kernel,
        grid=(t // tt,),
        in_specs=[pl.BlockSpec((tt, n), lambda i: (i, 0)),
                  pl.BlockSpec(keys.shape, lambda i: (0, 0, 0))],
        out_specs=[pl.BlockSpec((tt, PEER_SEL), lambda i: (i, 0)),
                   pl.BlockSpec((tt, PEER_SEL), lambda i: (i, 0))],
        out_shape=[jax.ShapeDtypeStruct((t, PEER_SEL), I32),
                   jax.ShapeDtypeStruct((t, PEER_SEL), F32)],
        compiler_params=_params(("arbitrary",)),
        name="peer_route",
    )(hq, keys)


def _unpack(slab):
    lo = pltpu.bitcast(slab << 16, F32)
    hi = pltpu.bitcast(slab & HI_MASK, F32)
    return lo, hi


def _peer_u_kernel(idx_ref, h_ref, g_ref, tab_ref, w_ref, tile_ref):
    tt = h_ref.shape[0]
    lane = lax.broadcasted_iota(I32, (PEER_SEL, LANES), 1)
    slot_rows = PEER_SEL * ROW_WORDS
    acc = jnp.zeros((PEER_SEL, LANES), F32)
    for t in range(tt):
        base = (t % PEER_U_SLOTS) * slot_rows
        hv = h_ref[t]
        ta, tb = hv[0:ROW_WORDS], hv[ROW_WORDS:2 * ROW_WORDS]
        for k in range(PEER_SEL):
            i = pl.multiple_of(idx_ref[t, k], ROW_WORDS)
            lo, hi = _unpack(tab_ref[pl.ds(i, ROW_WORDS), :])
            tile_ref[base + ROW_WORDS * k:base + ROW_WORDS * (k + 1), :] = lo * ta + hi * tb
        q = tile_ref[pl.ds(base, PEER_SEL, stride=ROW_WORDS), :]
        for s in range(1, ROW_WORDS):
            q = q + tile_ref[pl.ds(base + s, PEER_SEL, stride=ROW_WORDS), :]
        col = jnp.sum(q, axis=1, keepdims=True)
        acc = jnp.where(lane == t, col, acc)
    w_ref[...] = _gelu(acc.T[0:tt]) * g_ref[...]


def peer_u(idx, h3, g, tab):
    t = h3.shape[0]
    tt = PEER_TOKENS
    return pl.pallas_call(
        _peer_u_kernel,
        grid=(t // tt,),
        in_specs=[pl.BlockSpec((tt, PEER_SEL), lambda i: (i, 0), memory_space=pltpu.SMEM),
                  pl.BlockSpec((tt, SUBLANES, LANES), lambda i: (i, 0, 0)),
                  pl.BlockSpec((tt, PEER_SEL), lambda i: (i, 0)),
                  pl.BlockSpec(tab.shape, lambda i: (0, 0), pipeline_mode=pl.Buffered(1))],
        out_specs=pl.BlockSpec((tt, PEER_SEL), lambda i: (i, 0)),
        out_shape=jax.ShapeDtypeStruct((t, PEER_SEL), F32),
        scratch_shapes=[pltpu.VMEM((PEER_U_SLOTS * PEER_SEL * ROW_WORDS, LANES), F32)],
        compiler_params=_params(("arbitrary",), 48 * 1024 * 1024),
        name="peer_u",
    )(idx, h3, g, tab)


def _peer_v_kernel(idx_ref, w_ref, x_ref, gate_ref, expand_ref, cmask_ref, tab_ref, o_ref, tile_ref):
    tt = x_ref.shape[0]
    gate = gate_ref[...]
    slot_rows = PEER_SEL * ROW_WORDS
    wexp = jnp.dot(w_ref[...].astype(BF16), expand_ref[...], preferred_element_type=F32)
    cmask = cmask_ref[...]
    for t in range(tt):
        base = (t % PEER_V_SLOTS) * slot_rows
        for k in range(PEER_SEL):
            i = pl.multiple_of(idx_ref[t, k], ROW_WORDS)
            tile_ref[base + ROW_WORDS * k:base + ROW_WORDS * (k + 1), :] = tab_ref[pl.ds(i, ROW_WORDS), :]
        rows = pltpu.bitcast(tile_ref[base:base + slot_rows, :], BF16)
        lhs = (wexp[t:t + 1, :] * cmask).astype(BF16)
        y = jnp.dot(lhs, rows, preferred_element_type=F32)
        o_ref[t] = x_ref[t] + gate * y


def peer_v(idx, w, x3, mod5, tab, *, ctx, seq):
    t = x3.shape[0]
    tt = PEER_TOKENS
    row = (lambda i: 8) if ctx else (lambda i: (i * tt) // seq)
    halves = 2
    col = jnp.arange(PEER_SEL * ROW_WORDS * halves, dtype=I32)
    per_expert = ROW_WORDS * halves
    expand = (col[None, :] // per_expert == jnp.arange(PEER_SEL, dtype=I32)[:, None]).astype(BF16)
    chunk = (col % per_expert) // halves + ROW_WORDS * (col % halves)
    cmask = (chunk[None, :] == jnp.arange(SUBLANES, dtype=I32)[:, None]).astype(F32)
    return pl.pallas_call(
        _peer_v_kernel,
        grid=(t // tt,),
        in_specs=[pl.BlockSpec((tt, PEER_SEL), lambda i: (i, 0), memory_space=pltpu.SMEM),
                  pl.BlockSpec((tt, PEER_SEL), lambda i: (i, 0)),
                  pl.BlockSpec((tt, SUBLANES, LANES), lambda i: (i, 0, 0)),
                  pl.BlockSpec((None, None, SUBLANES, LANES), lambda i: (row(i), 5, 0, 0)),
                  pl.BlockSpec(expand.shape, lambda i: (0, 0)),
                  pl.BlockSpec(cmask.shape, lambda i: (0, 0)),
                  pl.BlockSpec(tab.shape, lambda i: (0, 0), pipeline_mode=pl.Buffered(1))],
        out_specs=pl.BlockSpec((tt, SUBLANES, LANES), lambda i: (i, 0, 0)),
        out_shape=jax.ShapeDtypeStruct(x3.shape, F32),
        scratch_shapes=[pltpu.VMEM((PEER_V_SLOTS * PEER_SEL * ROW_WORDS, LANES), I32)],
        compiler_params=_params(("arbitrary",), 48 * 1024 * 1024),
        name="peer_v",
    )(idx, w, x3, mod5, expand, cmask, tab)


def _pack_table(tab):
    e, d = tab.shape
    bits = lax.bitcast_convert_type(tab.astype(BF16), jnp.uint16).astype(jnp.uint32)
    word = bits[:, :d // 2] | (bits[:, d // 2:] << 16)
    return lax.bitcast_convert_type(word, I32).reshape(e * ROW_WORDS, LANES)


def _rope_tables(rows):
    r, col = jnp.meshgrid(jnp.arange(rows, dtype=F32), jnp.arange(GRID_W, dtype=F32), indexing="ij")
    freqs = ROPE_THETA ** (-jnp.arange(0, ROPE_AXIS_DIM, 2, dtype=F32) / ROPE_AXIS_DIM)
    ang_r = r.reshape(-1, 1) * freqs
    ang_c = col.reshape(-1, 1) * freqs
    cos = jnp.concatenate([jnp.cos(ang_r)] * 2 + [jnp.cos(ang_c)] * 2, axis=-1)
    sin = jnp.concatenate([-jnp.sin(ang_r), jnp.sin(ang_r), -jnp.sin(ang_c), jnp.sin(ang_c)], axis=-1)
    reps = HEADS * 2
    return jnp.tile(cos, (1, reps)), jnp.tile(sin, (1, reps))


def _dft(n):
    k = jnp.arange(n, dtype=I32)
    ang = ((k[:, None] * k[None, :]) % n).astype(F32) * (2.0 * math.pi / n)
    return jnp.cos(ang), jnp.sin(ang)


def _block_diag(m, groups):
    return jnp.kron(jnp.eye(groups, dtype=m.dtype), m)


def _peer(x, gain, mod4, w_q, keys, u_tab, v_tab, *, ctx):
    b, l, d = x.shape
    t = b * l
    h2, hq = normproj(x, gain, mod4, 3, 4, w_q, ctx=ctx, emit_h=True, out_dtype=F32)
    idx, g = peer_route(hq.reshape(t, -1), keys)
    h3 = h2.reshape(t, SUBLANES, LANES)
    w = peer_u(idx, h3, g, u_tab)
    mod5 = mod4.reshape(mod4.shape[0], 6, SUBLANES, LANES)
    out = peer_v(idx, w, x.reshape(t, SUBLANES, LANES), mod5, v_tab, ctx=ctx, seq=l)
    return out.reshape(b, l, d)


def kernel(x, c, ctx, c_ctx, w_ada, b_ada, norm1_g, norm2_g, w_in, b_gate, q_norm_g, k_norm_g,
           lam_params, subln_g, cm_ln_g, cm_w_s, cm_b_s, w_branch, w_out,
           peer_w_q, peer_sub_keys, peer_u_tab, peer_v_tab):
    bsz, seq, d = x.shape
    lc = ctx.shape[1]
    depth = w_ada.shape[0]
    assert bsz <= 8 and d == SUBLANES * LANES and seq % GRID_W == 0

    cin = jnp.zeros((16, d), F32).at[:bsz].set(c).at[bsz].set(c_ctx)
    mods = ada_mod(cin, w_ada, b_ada)

    cos_l, sin_l = _rope_tables(seq // GRID_W)
    cos_c = jnp.ones((lc, cos_l.shape[1]), F32)
    sin_c = jnp.zeros((lc, cos_l.shape[1]), F32)
    gmean = _block_diag(jnp.full((HEAD_DIM, HEAD_DIM), 1.0 / HEAD_DIM, F32), HEADS * 2).astype(BF16)
    cl_l, sl_l = (m.astype(BF16) for m in _dft(seq))
    cl_c, sl_c = (m.astype(BF16) for m in _dft(lc))
    cg, sg = _dft(FN_GROUP_W)
    cc = _block_diag(cg, FN_GROUPS).astype(BF16)
    sc = _block_diag(sg, FN_GROUPS).astype(BF16)
    qscale = HEAD_DIM ** -0.5

    xc = ctx
    for l in range(depth):
        last = l == depth - 1
        lam_init = 0.8 - 0.6 * math.exp(-0.3 * l)
        mod4 = mods[l].reshape(16, 6, 1, d)
        w_in_l = w_in[l].astype(BF16)
        reps = HEADS * 2
        gains = jnp.stack([jnp.tile(q_norm_g[l], reps) * qscale, jnp.tile(k_norm_g[l], reps)])[:, None, :]
        gains_c = jnp.stack([jnp.tile(q_norm_g[l], reps) * qscale, jnp.tile(k_norm_g[l], reps)])[:, None, :]
        ws = cm_w_s[l].astype(BF16)
        bias2d = jnp.repeat(cm_b_s[l].T, BRANCH_W // CM_GROUPS, axis=1)
        wb = w_branch[l].astype(BF16)
        wo = w_out[l].astype(BF16)
        wq = peer_w_q[l].astype(BF16)
        keys = peer_sub_keys[l].reshape(PEER_HEADS * 2, PEER_KEYS, -1).astype(BF16)
        u_tab = _pack_table(peer_u_tab[l])
        v_tab = _pack_table(peer_v_tab[l])

        (p,) = normproj(x, norm1_g[l], mod4, 0, 1, w_in_l, ctx=False, emit_h=False, out_dtype=BF16)
        (pc,) = normproj(xc, norm1_g[l], mod4, 0, 1, w_in_l, ctx=True, emit_h=False, out_dtype=BF16)
        qk = qkprep(p, gains, cos_l, sin_l, gmean)
        qkc = qkprep(pc, gains_c, cos_c, sin_c, gmean)
        k_all = jnp.concatenate([qkc[1], qk[1]], axis=1)
        v_all = jnp.concatenate([pc[..., 1024:1536], p[..., 1024:1536]], axis=1)
        att = diff_attention(qk[0], k_all, v_all, lam_params[l], subln_g[l], lam_init)
        m = chunk_mlp(p, cm_ln_g[l], ws, bias2d)
        f = fourier_mix(p, cl_l, sl_l, cc, sc)
        x = merge(att, m, f, p, b_gate[l], wb, wo, x, mod4, ctx=False)
        if not last:
            attc = diff_attention(qkc[0], qkc[1], pc[..., 1024:1536], lam_params[l], subln_g[l], lam_init)
            mc = chunk_mlp(pc, cm_ln_g[l], ws, bias2d)
            fc = fourier_mix(pc, cl_c, sl_c, cc, sc)
            xc = merge(attc, mc, fc, pc, b_gate[l], wb, wo, xc, mod4, ctx=True)

        x = _peer(x, norm2_g[l], mod4, wq, keys, u_tab, v_tab, ctx=False)
        if not last:
            xc = _peer(xc, norm2_g[l], mod4, wq, keys, u_tab, v_tab, ctx=True)
    return x
```

```python
import functools
import math

import jax
import jax.numpy as jnp
from jax import lax
from jax.experimental import pallas as pl
from jax.experimental.pallas import tpu as pltpu

F32 = jnp.float32
BF16 = jnp.bfloat16
I32 = jnp.int32

LANES = 128
SUBLANES = 8
VMEM_BYTES = 64 * 1024 * 1024

EPS = 1e-6
GRID_W = 64
HEADS = 4
HEAD_DIM = 64
ROPE_AXIS_DIM = HEAD_DIM // 2
ROPE_THETA = 10000.0
BRANCH_W = 512
CM_CHUNK = 128
CM_GROUPS = 4
FN_GROUPS = 4
FN_GROUP_W = BRANCH_W // FN_GROUPS
PEER_HEADS = 8
PEER_KEYS = 128
PEER_TOPK = 16
PEER_SEL = PEER_HEADS * PEER_TOPK
ROW_WORDS = 4
PEER_TOKENS = 32
PEER_U_SLOTS = 4
PEER_V_SLOTS = 4


def _vmem_limit(nbytes):
    return int(min(VMEM_BYTES - 4 * 1024 * 1024, max(32 * 1024 * 1024, nbytes)))


def _params(sem, vmem=None):
    return pltpu.CompilerParams(dimension_semantics=sem,
                                vmem_limit_bytes=_vmem_limit(vmem or 0))


def _gelu(x):
    return 0.5 * x * (1.0 + lax.erf(x * (1.0 / math.sqrt(2.0))))


def _ada_kernel(c_ref, w_ref, b_ref, o_ref):
    c = c_ref[...]
    a = c * jax.nn.sigmoid(c)
    o_ref[0] = jnp.dot(a.astype(BF16), w_ref[0].astype(BF16), preferred_element_type=F32) + b_ref[0]


def ada_mod(cin, w_ada, b_ada):
    depth, d, n = w_ada.shape
    r = cin.shape[0]
    tn = 1536
    return pl.pallas_call(
        _ada_kernel,
        grid=(depth, n // tn),
        in_specs=[pl.BlockSpec((r, d), lambda l, j: (0, 0)),
                  pl.BlockSpec((1, d, tn), lambda l, j: (l, 0, j)),
                  pl.BlockSpec((1, 1, tn), lambda l, j: (l, 0, j))],
        out_specs=pl.BlockSpec((1, r, tn), lambda l, j: (l, 0, j)),
        out_shape=jax.ShapeDtypeStruct((depth, r, n), F32),
        compiler_params=_params(("arbitrary", "arbitrary")),
        name="ada_mod",
    )(cin, w_ada, b_ada.reshape(depth, 1, n))


def _normproj_kernel(x_ref, g_ref, sh_ref, sc_ref, w_ref, *refs, emit_h):
    if emit_h:
        h_ref, p_ref, hs_ref = refs
    else:
        p_ref, hs_ref = refs

    @pl.when(pl.program_id(2) == 0)
    def _():
        x = x_ref[0]
        y = x * lax.rsqrt(jnp.mean(x * x, axis=-1, keepdims=True) + EPS) * g_ref[...]
        h = y * (1.0 + sc_ref[...]) + sh_ref[...]
        hs_ref[...] = h.astype(BF16)
        if emit_h:
            h_ref[0] = h

    p_ref[0] = jnp.dot(hs_ref[...], w_ref[...], preferred_element_type=F32).astype(p_ref.dtype)


def normproj(x, gain, mod4, shift_i, scale_i, w, *, ctx, emit_h, out_dtype):
    b, l, d = x.shape
    n = w.shape[1]
    tm = min(512, l)
    tn = 1536 if n % 1536 == 0 else 1024
    assert l % tm == 0 and n % tn == 0
    row = (lambda bi: 8) if ctx else (lambda bi: bi)
    in_specs = [pl.BlockSpec((1, tm, d), lambda bi, i, j: (bi, i, 0)),
                pl.BlockSpec((1, d), lambda bi, i, j: (0, 0)),
                pl.BlockSpec((None, None, 1, d), lambda bi, i, j: (row(bi), shift_i, 0, 0)),
                pl.BlockSpec((None, None, 1, d), lambda bi, i, j: (row(bi), scale_i, 0, 0)),
                pl.BlockSpec((d, tn), lambda bi, i, j: (0, j))]
    out_specs = [pl.BlockSpec((1, tm, tn), lambda bi, i, j: (bi, i, j))]
    out_shape = [jax.ShapeDtypeStruct((b, l, n), out_dtype)]
    if emit_h:
        out_specs = [pl.BlockSpec((1, tm, d), lambda bi, i, j: (bi, i, 0))] + out_specs
        out_shape = [jax.ShapeDtypeStruct((b, l, d), F32)] + out_shape
    return pl.pallas_call(
        functools.partial(_normproj_kernel, emit_h=emit_h),
        grid=(b, l // tm, n // tn),
        in_specs=in_specs, out_specs=out_specs, out_shape=out_shape,
        scratch_shapes=[pltpu.VMEM((tm, d), BF16)],
        compiler_params=_params(("arbitrary", "arbitrary", "arbitrary")),
        name="normproj",
    )(x, gain.reshape(1, d), mod4, mod4, w)


def _qkprep_kernel(p_ref, g_ref, cos_ref, sin_ref, gm_ref, o_ref):
    x = p_ref[0].astype(F32)
    ms = jnp.dot((x * x).astype(BF16), gm_ref[...], preferred_element_type=F32)
    y = x * lax.rsqrt(ms + EPS) * g_ref[0]
    w = y.shape[-1]
    lane = lax.broadcasted_iota(I32, y.shape, 1)
    half = ROPE_AXIS_DIM // 2
    partner = jnp.where((lane % ROPE_AXIS_DIM) < half,
                        pltpu.roll(y, w - half, axis=1), pltpu.roll(y, half, axis=1))
    o_ref[0, 0] = (y * cos_ref[...] + partner * sin_ref[...]).astype(o_ref.dtype)


def qkprep(p, gains, cos, sin, gmean):
    b, l, _ = p.shape
    w = HEADS * 2 * HEAD_DIM
    tm = min(512, l)
    return pl.pallas_call(
        _qkprep_kernel,
        grid=(2, b, l // tm),
        in_specs=[pl.BlockSpec((1, tm, w), lambda s, bi, i: (bi, i, s)),
                  pl.BlockSpec((1, 1, w), lambda s, bi, i: (s, 0, 0)),
                  pl.BlockSpec((tm, w), lambda s, bi, i: (i, 0)),
                  pl.BlockSpec((tm, w), lambda s, bi, i: (i, 0)),
                  pl.BlockSpec((w, w), lambda s, bi, i: (0, 0))],
        out_specs=pl.BlockSpec((1, 1, tm, w), lambda s, bi, i: (s, bi, i, 0)),
        out_shape=jax.ShapeDtypeStruct((2, b, l, w), BF16),
        compiler_params=_params(("arbitrary", "arbitrary", "arbitrary")),
        name="qkprep",
    )(p, gains, cos, sin, gmean)


def _attn_kernel(q_ref, k_ref, v_ref, lam_ref, g_ref, o_ref, *, lam_init):
    q = q_ref[0]
    k = k_ref[0]
    v = v_ref[0]
    lane = lax.broadcasted_iota(I32, q.shape, 1)
    zero = jnp.zeros_like(q)

    def softmax_v(qm):
        s = lax.dot_general(qm, k, (((1,), (1,)), ((), ())), preferred_element_type=F32)
        e = jnp.exp(s - jnp.max(s, axis=-1, keepdims=True))
        den = jnp.sum(e, axis=-1, keepdims=True)
        return jnp.dot(e.astype(BF16), v, preferred_element_type=F32) / den

    o1 = softmax_v(jnp.where(lane < HEAD_DIM, q, zero))
    o2 = softmax_v(jnp.where(lane >= HEAD_DIM, q, zero))
    lp = lam_ref[...]
    lam = (jnp.exp(jnp.sum(lp[0:1] * lp[1:2], axis=-1, keepdims=True))
           - jnp.exp(jnp.sum(lp[2:3] * lp[3:4], axis=-1, keepdims=True)) + lam_init)
    o = o1 - lam * o2
    y = o * lax.rsqrt(jnp.mean(o * o, axis=-1, keepdims=True) + EPS) * g_ref[...]
    o_ref[0] = (y * (1.0 - lam_init)).astype(o_ref.dtype)


def diff_attention(q, k, v, lam_params, subln_g, lam_init):
    b, lq, w = q.shape
    lk = k.shape[1]
    hw = w // HEADS
    tq = min(256, lq)
    return pl.pallas_call(
        functools.partial(_attn_kernel, lam_init=lam_init),
        grid=(b, HEADS, lq // tq),
        in_specs=[pl.BlockSpec((1, tq, hw), lambda bi, h, i: (bi, i, h)),
                  pl.BlockSpec((1, lk, hw), lambda bi, h, i: (bi, 0, h)),
                  pl.BlockSpec((1, lk, hw), lambda bi, h, i: (bi, 0, h)),
                  pl.BlockSpec((4, HEAD_DIM), lambda bi, h, i: (0, 0)),
                  pl.BlockSpec((1, hw), lambda bi, h, i: (0, 0))],
        out_specs=pl.BlockSpec((1, tq, hw), lambda bi, h, i: (bi, i, h)),
        out_shape=jax.ShapeDtypeStruct((b, lq, w), BF16),
        compiler_params=_params(("arbitrary", "arbitrary", "arbitrary"), 48 * 1024 * 1024),
        name="diff_attention",
    )(q, k, v, lam_params, subln_g.reshape(1, hw))


def _chunkmlp_kernel(zu_ref, zv_ref, g_ref, ws_ref, bias_ref, o_ref):
    u = _gelu(zu_ref[0].astype(F32))
    gv = _gelu(zv_ref[0].astype(F32))
    mu = jnp.mean(gv, axis=-1, keepdims=True)
    var = jnp.mean(jnp.square(gv - mu), axis=-1, keepdims=True)
    vn = ((gv - mu) * lax.rsqrt(var + EPS) * g_ref[...]).astype(BF16)
    tm = u.shape[0]
    gw = BRANCH_W // CM_GROUPS
    for n in range(tm // CM_CHUNK):
        rows = slice(n * CM_CHUNK, (n + 1) * CM_CHUNK)
        parts = [jnp.dot(ws_ref[g], vn[rows, g * gw:(g + 1) * gw], preferred_element_type=F32)
                 for g in range(CM_GROUPS)]
        s = jnp.concatenate(parts, axis=-1) + bias_ref[...]
        o_ref[0, rows, :] = (u[rows] * s).astype(o_ref.dtype)


def chunk_mlp(p, ln_g, w_s, bias2d):
    b, l, _ = p.shape
    w = BRANCH_W
    tm = min(512, l)
    return pl.pallas_call(
        _chunkmlp_kernel,
        grid=(b, l // tm),
        in_specs=[pl.BlockSpec((1, tm, w), lambda bi, i: (bi, i, 3)),
                  pl.BlockSpec((1, tm, w), lambda bi, i: (bi, i, 4)),
                  pl.BlockSpec((1, w), lambda bi, i: (0, 0)),
                  pl.BlockSpec((CM_GROUPS, CM_CHUNK, CM_CHUNK), lambda bi, i: (0, 0, 0)),
                  pl.BlockSpec((CM_CHUNK, w), lambda bi, i: (0, 0))],
        out_specs=pl.BlockSpec((1, tm, w), lambda bi, i: (bi, i, 0)),
        out_shape=jax.ShapeDtypeStruct((b, l, w), BF16),
        compiler_params=_params(("arbitrary", "arbitrary")),
        name="chunk_mlp",
    )(p, p, ln_g.reshape(1, w), w_s, bias2d)


def _fourier_kernel(z_ref, cl_ref, sl_ref, cc_ref, sc_ref, o_ref, y1_ref, y2_ref, *, scale):
    @pl.when(pl.program_id(1) == 0)
    def _():
        z = z_ref[0]
        y1_ref[...] = jnp.dot(z, cc_ref[...], preferred_element_type=F32).astype(BF16)
        y2_ref[...] = jnp.dot(z, sc_ref[...], preferred_element_type=F32).astype(BF16)

    acc = (jnp.dot(cl_ref[...], y1_ref[...], preferred_element_type=F32)
           - jnp.dot(sl_ref[...], y2_ref[...], preferred_element_type=F32))
    o_ref[0] = (acc * scale).astype(o_ref.dtype)


def fourier_mix(p, cl, sl, cc, sc):
    b, l, _ = p.shape
    w = BRANCH_W
    tm = min(512, l)
    scale = 1.0 / math.sqrt(l * FN_GROUP_W)
    return pl.pallas_call(
        functools.partial(_fourier_kernel, scale=scale),
        grid=(b, l // tm),
        in_specs=[pl.BlockSpec((1, l, w), lambda bi, i: (bi, 0, 5)),
                  pl.BlockSpec((tm, l), lambda bi, i: (i, 0)),
                  pl.BlockSpec((tm, l), lambda bi, i: (i, 0)),
                  pl.BlockSpec((w, w), lambda bi, i: (0, 0)),
                  pl.BlockSpec((w, w), lambda bi, i: (0, 0))],
        out_specs=pl.BlockSpec((1, tm, w), lambda bi, i: (bi, i, 0)),
        out_shape=jax.ShapeDtypeStruct((b, l, w), BF16),
        scratch_shapes=[pltpu.VMEM((l, w), BF16), pltpu.VMEM((l, w), BF16)],
        compiler_params=_params(("arbitrary", "arbitrary"), 48 * 1024 * 1024),
        name="fourier_mix",
    )(p, cl, sl, cc, sc)


def _merge_kernel(a_ref, m_ref, f_ref, gl0_ref, gl1_ref, gl2_ref, bg_ref, wb_ref, wo_ref,
                  x_ref, gate_ref, o_ref):
    s = None
    for n, (br, gl) in enumerate(((a_ref, gl0_ref), (m_ref, gl1_ref), (f_ref, gl2_ref))):
        y = jnp.dot(br[0], wb_ref[n], preferred_element_type=F32)
        t = jax.nn.sigmoid(gl[0].astype(F32) + bg_ref[n]) * y
        s = t if s is None else s + t
    y = jnp.dot(s.astype(BF16), wo_ref[...], preferred_element_type=F32)
    o_ref[0] = x_ref[0] + gate_ref[...] * y


def merge(att, m, f, p, b_gate, w_branch, w_out, x, mod4, *, ctx):
    b, l, d = x.shape
    w = BRANCH_W
    tm = min(256, l)
    row = (lambda bi: 8) if ctx else (lambda bi: bi)
    br_spec = pl.BlockSpec((1, tm, w), lambda bi, i: (bi, i, 0))
    gl_specs = [pl.BlockSpec((1, tm, d), lambda bi, i, n=n: (bi, i, 3 + n)) for n in range(3)]
    return pl.pallas_call(
        _merge_kernel,
        grid=(b, l // tm),
        in_specs=[br_spec, br_spec, br_spec] + gl_specs + [
            pl.BlockSpec((3, 1, d), lambda bi, i: (0, 0, 0)),
            pl.BlockSpec((3, w, d), lambda bi, i: (0, 0, 0)),
            pl.BlockSpec((d, d), lambda bi, i: (0, 0)),
            pl.BlockSpec((1, tm, d), lambda bi, i: (bi, i, 0)),
            pl.BlockSpec((None, None, 1, d), lambda bi, i: (row(bi), 2, 0, 0))],
        out_specs=pl.BlockSpec((1, tm, d), lambda bi, i: (bi, i, 0)),
        out_shape=jax.ShapeDtypeStruct((b, l, d), F32),
        compiler_params=_params(("arbitrary", "arbitrary")),
        name="merge",
    )(att, m, f, p, p, p, b_gate.reshape(3, 1, d), w_branch, w_out, x, mod4)


def _top_rows(s, k, carry=None):
    n = s.shape[0]
    row = lax.broadcasted_iota(I32, s.shape, 0)
    vals, picks = [], []
    for _ in range(k):
        m = jnp.max(s, axis=0, keepdims=True)
        pos = jnp.min(jnp.where(s == m, row, n), axis=0, keepdims=True)
        hit = row == pos
        vals.append(m)
        if carry is None:
            picks.append(pos)
        else:
            picks.append(jnp.sum(jnp.where(hit, carry, 0), axis=0, keepdims=True))
        s = jnp.where(hit, -jnp.inf, s)
    return jnp.concatenate(vals, axis=0), jnp.concatenate(picks, axis=0)


def _candidates(s1, i1, s2, i2):
    k = s1.shape[0]
    ps, pi = [], []
    i = 0
    while k // (i + 1) > 1:
        n = k // (i + 1)
        rows = -(-n // SUBLANES) * SUBLANES
        v = s1[i:i + 1] + s2[0:rows]
        if n < rows:
            v = jnp.where(lax.broadcasted_iota(I32, v.shape, 0) < n, v, -jnp.inf)
        ps.append(v)
        pi.append(i1[i:i + 1] * PEER_KEYS + i2[0:rows])
        i += 1
    ps.append(s1[i:] + s2[0:1])
    pi.append(i1[i:] * PEER_KEYS + i2[0:1])
    return jnp.concatenate(ps, axis=0), jnp.concatenate(pi, axis=0)


def _route_kernel(hq_ref, keys_ref, idx_ref, g_ref):
    half = PEER_KEYS
    picks, gates = [], []
    for h in range(PEER_HEADS):
        tops = []
        for part in range(2):
            c = (2 * h + part) * half
            qc = hq_ref[:, c:c + half]
            qn = qc * lax.rsqrt(jnp.mean(qc * qc, axis=-1, keepdims=True) + EPS)
            st = lax.dot_general(keys_ref[2 * h + part], qn.astype(BF16),
                                 (((1,), (1,)), ((), ())), preferred_element_type=F32)
            tops.append(_top_rows(st, PEER_TOPK))
        cand_s, cand_i = _candidates(*tops[0], *tops[1])
        top_s, top_i = _top_rows(cand_s, PEER_TOPK, carry=cand_i)
        e = jnp.exp(top_s - top_s[0:1])
        gates.append(e / jnp.sum(e, axis=0, keepdims=True))
        picks.append(top_i * ROW_WORDS)
    idx_ref[...] = jnp.concatenate(picks, axis=0).T
    g_ref[...] = jnp.concatenate(gates, axis=0).T


def peer_route(hq, keys):
    t, n = hq.shape
    tt = min(256, t)
    return pl.pallas_call(
        _route_kernel,
        grid=(t // tt,),
        in_specs=[pl.BlockSpec((tt, n), lambda i: (i, 0)),
                  pl.BlockSpec(keys.shape, lambda i: (0, 0, 0))],
        out_specs=[pl.BlockSpec((tt, PEER_SEL), lambda i: (i, 0)),
                   pl.BlockSpec((tt, PEER_SEL), lambda i: (i, 0))],
        out_shape=[jax.ShapeDtypeStruct((t, PEER_SEL), I32),
                   jax.ShapeDtypeStruct((t, PEER_SEL), F32)],
        compiler_params=_params(("arbitrary",)),
        name="peer_route",
    )(hq, keys)


def _chunk_tables():
    halves = 2
    per_expert = ROW_WORDS * halves
    col = jnp.arange(PEER_SEL * per_expert, dtype=I32)
    expand = (col[None, :] // per_expert == jnp.arange(PEER_SEL, dtype=I32)[:, None]).astype(BF16)
    chunk = (col % per_expert) // halves + ROW_WORDS * (col % halves)
    cmask = (chunk[None, :] == jnp.arange(SUBLANES, dtype=I32)[:, None]).astype(F32)
    return expand, cmask


def _fetch_rows(idx_ref, t, tab_ref, tile_ref, base):
    for k in range(PEER_SEL):
        i = pl.multiple_of(idx_ref[t, k], ROW_WORDS)
        tile_ref[base + ROW_WORDS * k:base + ROW_WORDS * (k + 1), :] = tab_ref[pl.ds(i, ROW_WORDS), :]


def _peer_u_kernel(idx_ref, h_ref, g_ref, gather_ref, cmask_ref, tab_ref, w_ref, tile_ref, z_ref):
    tt = h_ref.shape[0]
    slot_rows = PEER_SEL * ROW_WORDS
    cmask = cmask_ref[...]
    for t in range(tt):
        base = (t % PEER_U_SLOTS) * slot_rows
        _fetch_rows(idx_ref, t, tab_ref, tile_ref, base)
        rows = pltpu.bitcast(tile_ref[base:base + slot_rows, :], BF16)
        hb = jnp.concatenate([h_ref[t:t + 1, LANES * c:LANES * (c + 1)] for c in range(SUBLANES)],
                             axis=0).astype(BF16)
        r = lax.dot_general(hb, rows, (((1,), (1,)), ((), ())), preferred_element_type=F32)
        z_ref[t:t + 1, :] = jnp.sum(r * cmask, axis=0, keepdims=True)
    z = z_ref[...]
    zh = z.astype(BF16)
    zl = (z - zh.astype(F32)).astype(BF16)
    act = (jnp.dot(zh, gather_ref[...], preferred_element_type=F32)
           + jnp.dot(zl, gather_ref[...], preferred_element_type=F32))
    w_ref[...] = _gelu(act) * g_ref[...]


def peer_u(idx, h, g, tab):
    t, d = h.shape
    tt = PEER_TOKENS
    expand, cmask = _chunk_tables()
    gather = expand.T
    return pl.pallas_call(
        _peer_u_kernel,
        grid=(t // tt,),
        in_specs=[pl.BlockSpec((tt, PEER_SEL), lambda i: (i, 0), memory_space=pltpu.SMEM),
                  pl.BlockSpec((tt, d), lambda i: (i, 0)),
                  pl.BlockSpec((tt, PEER_SEL), lambda i: (i, 0)),
                  pl.BlockSpec(gather.shape, lambda i: (0, 0)),
                  pl.BlockSpec(cmask.shape, lambda i: (0, 0)),
                  pl.BlockSpec(tab.shape, lambda i: (0, 0), pipeline_mode=pl.Buffered(1))],
        out_specs=pl.BlockSpec((tt, PEER_SEL), lambda i: (i, 0)),
        out_shape=jax.ShapeDtypeStruct((t, PEER_SEL), F32),
        scratch_shapes=[pltpu.VMEM((PEER_U_SLOTS * PEER_SEL * ROW_WORDS, LANES), I32),
                        pltpu.VMEM((tt, PEER_SEL * SUBLANES), F32)],
        compiler_params=_params(("arbitrary",), 48 * 1024 * 1024),
        name="peer_u",
    )(idx, h, g, gather, cmask, tab)


def _peer_v_kernel(idx_ref, w_ref, x_ref, gate_ref, expand_ref, cmask_ref, tab_ref, o_ref, tile_ref):
    tt = x_ref.shape[0]
    slot_rows = PEER_SEL * ROW_WORDS
    wexp = jnp.dot(w_ref[...].astype(BF16), expand_ref[...], preferred_element_type=F32)
    cmask = cmask_ref[...]
    for t in range(tt):
        base = (t % PEER_V_SLOTS) * slot_rows
        _fetch_rows(idx_ref, t, tab_ref, tile_ref, base)
        rows = pltpu.bitcast(tile_ref[base:base + slot_rows, :], BF16)
        lhs = (wexp[t:t + 1, :] * cmask).astype(BF16)
        y = jnp.dot(lhs, rows, preferred_element_type=F32)
        for c in range(SUBLANES):
            cols = slice(LANES * c, LANES * (c + 1))
            o_ref[t:t + 1, cols] = x_ref[t:t + 1, cols] + gate_ref[:, cols] * y[c:c + 1, :]


def peer_v(idx, w, x, mod4, tab, *, ctx, seq):
    t, d = x.shape
    tt = PEER_TOKENS
    row = (lambda i: 8) if ctx else (lambda i: (i * tt) // seq)
    expand, cmask = _chunk_tables()
    return pl.pallas_call(
        _peer_v_kernel,
        grid=(t // tt,),
        in_specs=[pl.BlockSpec((tt, PEER_SEL), lambda i: (i, 0), memory_space=pltpu.SMEM),
                  pl.BlockSpec((tt, PEER_SEL), lambda i: (i, 0)),
                  pl.BlockSpec((tt, d), lambda i: (i, 0)),
                  pl.BlockSpec((None, None, 1, d), lambda i: (row(i), 5, 0, 0)),
                  pl.BlockSpec(expand.shape, lambda i: (0, 0)),
                  pl.BlockSpec(cmask.shape, lambda i: (0, 0)),
                  pl.BlockSpec(tab.shape, lambda i: (0, 0), pipeline_mode=pl.Buffered(1))],
        out_specs=pl.BlockSpec((tt, d), lambda i: (i, 0)),
        out_shape=jax.ShapeDtypeStruct(x.shape, F32),
        scratch_shapes=[pltpu.VMEM((PEER_V_SLOTS * PEER_SEL * ROW_WORDS, LANES), I32)],
        compiler_params=_params(("arbitrary",), 48 * 1024 * 1024),
        name="peer_v",
    )(idx, w, x, mod4, expand, cmask, tab)


def _pack_table(tab):
    e, d = tab.shape
    bits = lax.bitcast_convert_type(tab.astype(BF16), jnp.uint16).astype(jnp.uint32)
    word = bits[:, :d // 2] | (bits[:, d // 2:] << 16)
    return lax.bitcast_convert_type(word, I32).reshape(e * ROW_WORDS, LANES)


def _rope_tables(rows):
    r, col = jnp.meshgrid(jnp.arange(rows, dtype=F32), jnp.arange(GRID_W, dtype=F32), indexing="ij")
    freqs = ROPE_THETA ** (-jnp.arange(0, ROPE_AXIS_DIM, 2, dtype=F32) / ROPE_AXIS_DIM)
    ang_r = r.reshape(-1, 1) * freqs
    ang_c = col.reshape(-1, 1) * freqs
    cos = jnp.concatenate([jnp.cos(ang_r)] * 2 + [jnp.cos(ang_c)] * 2, axis=-1)
    sin = jnp.concatenate([-jnp.sin(ang_r), jnp.sin(ang_r), -jnp.sin(ang_c), jnp.sin(ang_c)], axis=-1)
    reps = HEADS * 2
    return jnp.tile(cos, (1, reps)), jnp.tile(sin, (1, reps))


def _dft(n):
    k = jnp.arange(n, dtype=I32)
    ang = ((k[:, None] * k[None, :]) % n).astype(F32) * (2.0 * math.pi / n)
    return jnp.cos(ang), jnp.sin(ang)


def _block_diag(m, groups):
    return jnp.kron(jnp.eye(groups, dtype=m.dtype), m)


def _peer(x, gain, mod4, w_q, keys, u_tab, v_tab, *, ctx):
    b, l, d = x.shape
    t = b * l
    h2, hq = normproj(x, gain, mod4, 3, 4, w_q, ctx=ctx, emit_h=True, out_dtype=F32)
    idx, g = peer_route(hq.reshape(t, -1), keys)
    w = peer_u(idx, h2.reshape(t, d), g, u_tab)
    out = peer_v(idx, w, x.reshape(t, d), mod4, v_tab, ctx=ctx, seq=l)
    return out.reshape(b, l, d)


def kernel(x, c, ctx, c_ctx, w_ada, b_ada, norm1_g, norm2_g, w_in, b_gate, q_norm_g, k_norm_g,
           lam_params, subln_g, cm_ln_g, cm_w_s, cm_b_s, w_branch, w_out,
           peer_w_q, peer_sub_keys, peer_u_tab, peer_v_tab):
    bsz, seq, d = x.shape
    lc = ctx.shape[1]
    depth = w_ada.shape[0]
    assert bsz <= 8 and d == SUBLANES * LANES and seq % GRID_W == 0

    cin = jnp.zeros((16, d), F32).at[:bsz].set(c).at[bsz].set(c_ctx)
    mods = ada_mod(cin, w_ada, b_ada)

    cos_l, sin_l = _rope_tables(seq // GRID_W)
    cos_c = jnp.ones((lc, cos_l.shape[1]), F32)
    sin_c = jnp.zeros((lc, cos_l.shape[1]), F32)
    gmean = _block_diag(jnp.full((HEAD_DIM, HEAD_DIM), 1.0 / HEAD_DIM, F32), HEADS * 2).astype(BF16)
    cl_l, sl_l = (m.astype(BF16) for m in _dft(seq))
    cl_c, sl_c = (m.astype(BF16) for m in _dft(lc))
    cg, sg = _dft(FN_GROUP_W)
    cc = _block_diag(cg, FN_GROUPS).astype(BF16)
    sc = _block_diag(sg, FN_GROUPS).astype(BF16)
    qscale = HEAD_DIM ** -0.5

    xc = ctx
    for l in range(depth):
        last = l == depth - 1
        lam_init = 0.8 - 0.6 * math.exp(-0.3 * l)
        mod4 = mods[l].reshape(16, 6, 1, d)
        w_in_l = w_in[l].astype(BF16)
        reps = HEADS * 2
        gains = jnp.stack([jnp.tile(q_norm_g[l], reps) * qscale, jnp.tile(k_norm_g[l], reps)])[:, None, :]
        gains_c = jnp.stack([jnp.tile(q_norm_g[l], reps) * qscale, jnp.tile(k_norm_g[l], reps)])[:, None, :]
        ws = cm_w_s[l].astype(BF16)
        bias2d = jnp.repeat(cm_b_s[l].T, BRANCH_W // CM_GROUPS, axis=1)
        wb = w_branch[l].astype(BF16)
        wo = w_out[l].astype(BF16)
        wq = peer_w_q[l].astype(BF16)
        keys = peer_sub_keys[l].reshape(PEER_HEADS * 2, PEER_KEYS, -1).astype(BF16)
        u_tab = _pack_table(peer_u_tab[l])
        v_tab = _pack_table(peer_v_tab[l])

        (p,) = normproj(x, norm1_g[l], mod4, 0, 1, w_in_l, ctx=False, emit_h=False, out_dtype=BF16)
        (pc,) = normproj(xc, norm1_g[l], mod4, 0, 1, w_in_l, ctx=True, emit_h=False, out_dtype=BF16)
        qk = qkprep(p, gains, cos_l, sin_l, gmean)
        qkc = qkprep(pc, gains_c, cos_c, sin_c, gmean)
        k_all = jnp.concatenate([qkc[1], qk[1]], axis=1)
        v_all = jnp.concatenate([pc[..., 1024:1536], p[..., 1024:1536]], axis=1)
        att = diff_attention(qk[0], k_all, v_all, lam_params[l], subln_g[l], lam_init)
        m = chunk_mlp(p, cm_ln_g[l], ws, bias2d)
        f = fourier_mix(p, cl_l, sl_l, cc, sc)
        x = merge(att, m, f, p, b_gate[l], wb, wo, x, mod4, ctx=False)
        if not last:
            attc = diff_attention(qkc[0], qkc[1], pc[..., 1024:1536], lam_params[l], subln_g[l], lam_init)
            mc = chunk_mlp(pc, cm_ln_g[l], ws, bias2d)
            fc = fourier_mix(pc, cl_c, sl_c, cc, sc)
            xc = merge(attc, mc, fc, pc, b_gate[l], wb, wo, xc, mod4, ctx=True)

        x = _peer(x, norm2_g[l], mod4, wq, keys, u_tab, v_tab, ctx=False)
        if not last:
            xc = _peer(xc, norm2_g[l], mod4, wq, keys, u_tab, v_tab, ctx=True)
    return x
```

```python
import functools
import math

import jax
import jax.numpy as jnp
from jax import lax
from jax.experimental import pallas as pl
from jax.experimental.pallas import tpu as pltpu

F32 = jnp.float32
BF16 = jnp.bfloat16
I32 = jnp.int32

LANES = 128
SUBLANES = 8
VMEM_BYTES = 64 * 1024 * 1024

EPS = 1e-6
GRID_W = 64
HEADS = 4
HEAD_DIM = 64
ROPE_AXIS_DIM = HEAD_DIM // 2
ROPE_THETA = 10000.0
BRANCH_W = 512
CM_CHUNK = 128
CM_GROUPS = 4
FN_GROUPS = 4
FN_GROUP_W = BRANCH_W // FN_GROUPS
PEER_HEADS = 8
PEER_KEYS = 128
PEER_TOPK = 16
PEER_SEL = PEER_HEADS * PEER_TOPK
ROW_WORDS = 4
PEER_TOKENS = 32
PEER_U_SLOTS = 4
PEER_V_SLOTS = 4


def _vmem_limit(nbytes):
    return int(min(VMEM_BYTES - 4 * 1024 * 1024, max(32 * 1024 * 1024, nbytes)))


def _params(sem, vmem=None):
    return pltpu.CompilerParams(dimension_semantics=sem,
                                vmem_limit_bytes=_vmem_limit(vmem or 0))


def _gelu(x):
    return 0.5 * x * (1.0 + lax.erf(x * (1.0 / math.sqrt(2.0))))


def _ada_kernel(c_ref, w_ref, b_ref, o_ref):
    c = c_ref[...]
    a = c * jax.nn.sigmoid(c)
    o_ref[0] = jnp.dot(a.astype(BF16), w_ref[0].astype(BF16), preferred_element_type=F32) + b_ref[0]


def ada_mod(cin, w_ada, b_ada):
    depth, d, n = w_ada.shape
    r = cin.shape[0]
    tn = 1536
    return pl.pallas_call(
        _ada_kernel,
        grid=(depth, n // tn),
        in_specs=[pl.BlockSpec((r, d), lambda l, j: (0, 0)),
                  pl.BlockSpec((1, d, tn), lambda l, j: (l, 0, j)),
                  pl.BlockSpec((1, 1, tn), lambda l, j: (l, 0, j))],
        out_specs=pl.BlockSpec((1, r, tn), lambda l, j: (l, 0, j)),
        out_shape=jax.ShapeDtypeStruct((depth, r, n), F32),
        compiler_params=_params(("arbitrary", "arbitrary")),
        name="ada_mod",
    )(cin, w_ada, b_ada.reshape(depth, 1, n))


def _normproj_kernel(x_ref, g_ref, sh_ref, sc_ref, w_ref, *refs, emit_h):
    if emit_h:
        h_ref, p_ref, hs_ref = refs
    else:
        p_ref, hs_ref = refs

    @pl.when(pl.program_id(2) == 0)
    def _():
        x = x_ref[0]
        y = x * lax.rsqrt(jnp.mean(x * x, axis=-1, keepdims=True) + EPS) * g_ref[...]
        h = y * (1.0 + sc_ref[...]) + sh_ref[...]
        hs_ref[...] = h.astype(BF16)
        if emit_h:
            h_ref[0] = h

    p_ref[0] = jnp.dot(hs_ref[...], w_ref[...], preferred_element_type=F32).astype(p_ref.dtype)


def normproj(x, gain, mod4, shift_i, scale_i, w, *, ctx, emit_h, out_dtype):
    b, l, d = x.shape
    n = w.shape[1]
    tm = min(512, l)
    tn = 1536 if n % 1536 == 0 else 1024
    assert l % tm == 0 and n % tn == 0
    row = (lambda bi: 8) if ctx else (lambda bi: bi)
    in_specs = [pl.BlockSpec((1, tm, d), lambda bi, i, j: (bi, i, 0)),
                pl.BlockSpec((1, d), lambda bi, i, j: (0, 0)),
                pl.BlockSpec((None, None, 1, d), lambda bi, i, j: (row(bi), shift_i, 0, 0)),
                pl.BlockSpec((None, None, 1, d), lambda bi, i, j: (row(bi), scale_i, 0, 0)),
                pl.BlockSpec((d, tn), lambda bi, i, j: (0, j))]
    out_specs = [pl.BlockSpec((1, tm, tn), lambda bi, i, j: (bi, i, j))]
    out_shape = [jax.ShapeDtypeStruct((b, l, n), out_dtype)]
    if emit_h:
        out_specs = [pl.BlockSpec((1, tm, d), lambda bi, i, j: (bi, i, 0))] + out_specs
        out_shape = [jax.ShapeDtypeStruct((b, l, d), F32)] + out_shape
    return pl.pallas_call(
        functools.partial(_normproj_kernel, emit_h=emit_h),
        grid=(b, l // tm, n // tn),
        in_specs=in_specs, out_specs=out_specs, out_shape=out_shape,
        scratch_shapes=[pltpu.VMEM((tm, d), BF16)],
        compiler_params=_params(("arbitrary", "arbitrary", "arbitrary")),
        name="normproj",
    )(x, gain.reshape(1, d), mod4, mod4, w)


def _qkprep_kernel(p_ref, g_ref, cos_ref, sin_ref, gm_ref, o_ref):
    x = p_ref[0].astype(F32)
    ms = jnp.dot((x * x).astype(BF16), gm_ref[...], preferred_element_type=F32)
    y = x * lax.rsqrt(ms + EPS) * g_ref[0]
    w = y.shape[-1]
    lane = lax.broadcasted_iota(I32, y.shape, 1)
    half = ROPE_AXIS_DIM // 2
    partner = jnp.where((lane % ROPE_AXIS_DIM) < half,
                        pltpu.roll(y, w - half, axis=1), pltpu.roll(y, half, axis=1))
    o_ref[0, 0] = (y * cos_ref[...] + partner * sin_ref[...]).astype(o_ref.dtype)


def qkprep(p, gains, cos, sin, gmean):
    b, l, _ = p.shape
    w = HEADS * 2 * HEAD_DIM
    tm = min(512, l)
    return pl.pallas_call(
        _qkprep_kernel,
        grid=(2, b, l // tm),
        in_specs=[pl.BlockSpec((1, tm, w), lambda s, bi, i: (bi, i, s)),
                  pl.BlockSpec((1, 1, w), lambda s, bi, i: (s, 0, 0)),
                  pl.BlockSpec((tm, w), lambda s, bi, i: (i, 0)),
                  pl.BlockSpec((tm, w), lambda s, bi, i: (i, 0)),
                  pl.BlockSpec((w, w), lambda s, bi, i: (0, 0))],
        out_specs=pl.BlockSpec((1, 1, tm, w), lambda s, bi, i: (s, bi, i, 0)),
        out_shape=jax.ShapeDtypeStruct((2, b, l, w), BF16),
        compiler_params=_params(("arbitrary", "arbitrary", "arbitrary")),
        name="qkprep",
    )(p, gains, cos, sin, gmean)


def _attn_kernel(q_ref, k_ref, v_ref, lam_ref, g_ref, o_ref, *, lam_init):
    q = q_ref[0]
    k = k_ref[0]
    v = v_ref[0]
    lane = lax.broadcasted_iota(I32, q.shape, 1)
    zero = jnp.zeros_like(q)

    def softmax_v(qm):
        s = lax.dot_general(qm, k, (((1,), (1,)), ((), ())), preferred_element_type=F32)
        e = jnp.exp(s - jnp.max(s, axis=-1, keepdims=True))
        den = jnp.sum(e, axis=-1, keepdims=True)
        return jnp.dot(e.astype(BF16), v, preferred_element_type=F32) / den

    o1 = softmax_v(jnp.where(lane < HEAD_DIM, q, zero))
    o2 = softmax_v(jnp.where(lane >= HEAD_DIM, q, zero))
    lp = lam_ref[...]
    lam = (jnp.exp(jnp.sum(lp[0:1] * lp[1:2], axis=-1, keepdims=True))
           - jnp.exp(jnp.sum(lp[2:3] * lp[3:4], axis=-1, keepdims=True)) + lam_init)
    o = o1 - lam * o2
    y = o * lax.rsqrt(jnp.mean(o * o, axis=-1, keepdims=True) + EPS) * g_ref[...]
    o_ref[0] = (y * (1.0 - lam_init)).astype(o_ref.dtype)


def diff_attention(q, k, v, lam_params, subln_g, lam_init):
    b, lq, w = q.shape
    lk = k.shape[1]
    hw = w // HEADS
    tq = min(256, lq)
    return pl.pallas_call(
        functools.partial(_attn_kernel, lam_init=lam_init),
        grid=(b, HEADS, lq // tq),
        in_specs=[pl.BlockSpec((1, tq, hw), lambda bi, h, i: (bi, i, h)),
                  pl.BlockSpec((1, lk, hw), lambda bi, h, i: (bi, 0, h)),
                  pl.BlockSpec((1, lk, hw), lambda bi, h, i: (bi, 0, h)),
                  pl.BlockSpec((4, HEAD_DIM), lambda bi, h, i: (0, 0)),
                  pl.BlockSpec((1, hw), lambda bi, h, i: (0, 0))],
        out_specs=pl.BlockSpec((1, tq, hw), lambda bi, h, i: (bi, i, h)),
        out_shape=jax.ShapeDtypeStruct((b, lq, w), BF16),
        compiler_params=_params(("arbitrary", "arbitrary", "arbitrary"), 48 * 1024 * 1024),
        name="diff_attention",
    )(q, k, v, lam_params, subln_g.reshape(1, hw))


def _chunkmlp_kernel(zu_ref, zv_ref, g_ref, ws_ref, bias_ref, o_ref):
    u = _gelu(zu_ref[0].astype(F32))
    gv = _gelu(zv_ref[0].astype(F32))
    mu = jnp.mean(gv, axis=-1, keepdims=True)
    var = jnp.mean(jnp.square(gv - mu), axis=-1, keepdims=True)
    vn = ((gv - mu) * lax.rsqrt(var + EPS) * g_ref[...]).astype(BF16)
    tm = u.shape[0]
    gw = BRANCH_W // CM_GROUPS
    for n in range(tm // CM_CHUNK):
        rows = slice(n * CM_CHUNK, (n + 1) * CM_CHUNK)
        parts = [jnp.dot(ws_ref[g], vn[rows, g * gw:(g + 1) * gw], preferred_element_type=F32)
                 for g in range(CM_GROUPS)]
        s = jnp.concatenate(parts, axis=-1) + bias_ref[...]
        o_ref[0, rows, :] = (u[rows] * s).astype(o_ref.dtype)


def chunk_mlp(p, ln_g, w_s, bias2d):
    b, l, _ = p.shape
    w = BRANCH_W
    tm = min(512, l)
    return pl.pallas_call(
        _chunkmlp_kernel,
        grid=(b, l // tm),
        in_specs=[pl.BlockSpec((1, tm, w), lambda bi, i: (bi, i, 3)),
                  pl.BlockSpec((1, tm, w), lambda bi, i: (bi, i, 4)),
                  pl.BlockSpec((1, w), lambda bi, i: (0, 0)),
                  pl.BlockSpec((CM_GROUPS, CM_CHUNK, CM_CHUNK), lambda bi, i: (0, 0, 0)),
                  pl.BlockSpec((CM_CHUNK, w), lambda bi, i: (0, 0))],
        out_specs=pl.BlockSpec((1, tm, w), lambda bi, i: (bi, i, 0)),
        out_shape=jax.ShapeDtypeStruct((b, l, w), BF16),
        compiler_params=_params(("arbitrary", "arbitrary")),
        name="chunk_mlp",
    )(p, p, ln_g.reshape(1, w), w_s, bias2d)


def _fourier_kernel(z_ref, cl_ref, sl_ref, cc_ref, sc_ref, o_ref, y1_ref, y2_ref, *, scale):
    @pl.when(pl.program_id(1) == 0)
    def _():
        z = z_ref[0]
        y1_ref[...] = jnp.dot(z, cc_ref[...], preferred_element_type=F32).astype(BF16)
        y2_ref[...] = jnp.dot(z, sc_ref[...], preferred_element_type=F32).astype(BF16)

    acc = (jnp.dot(cl_ref[...], y1_ref[...], preferred_element_type=F32)
           - jnp.dot(sl_ref[...], y2_ref[...], preferred_element_type=F32))
    o_ref[0] = (acc * scale).astype(o_ref.dtype)


def fourier_mix(p, cl, sl, cc, sc):
    b, l, _ = p.shape
    w = BRANCH_W
    tm = min(512, l)
    scale = 1.0 / math.sqrt(l * FN_GROUP_W)
    return pl.pallas_call(
        functools.partial(_fourier_kernel, scale=scale),
        grid=(b, l // tm),
        in_specs=[pl.BlockSpec((1, l, w), lambda bi, i: (bi, 0, 5)),
                  pl.BlockSpec((tm, l), lambda bi, i: (i, 0)),
                  pl.BlockSpec((tm, l), lambda bi, i: (i, 0)),
                  pl.BlockSpec((w, w), lambda bi, i: (0, 0)),
                  pl.BlockSpec((w, w), lambda bi, i: (0, 0))],
        out_specs=pl.BlockSpec((1, tm, w), lambda bi, i: (bi, i, 0)),
        out_shape=jax.ShapeDtypeStruct((b, l, w), BF16),
        scratch_shapes=[pltpu.VMEM((l, w), BF16), pltpu.VMEM((l, w), BF16)],
        compiler_params=_params(("arbitrary", "arbitrary"), 48 * 1024 * 1024),
        name="fourier_mix",
    )(p, cl, sl, cc, sc)


def _merge_kernel(a_ref, m_ref, f_ref, gl0_ref, gl1_ref, gl2_ref, bg_ref, wb_ref, wo_ref,
                  x_ref, gate_ref, o_ref):
    s = None
    for n, (br, gl) in enumerate(((a_ref, gl0_ref), (m_ref, gl1_ref), (f_ref, gl2_ref))):
        y = jnp.dot(br[0], wb_ref[n], preferred_element_type=F32)
        t = jax.nn.sigmoid(gl[0].astype(F32) + bg_ref[n]) * y
        s = t if s is None else s + t
    y = jnp.dot(s.astype(BF16), wo_ref[...], preferred_element_type=F32)
    o_ref[0] = x_ref[0] + gate_ref[...] * y


def merge(att, m, f, p, b_gate, w_branch, w_out, x, mod4, *, ctx):
    b, l, d = x.shape
    w = BRANCH_W
    tm = min(256, l)
    row = (lambda bi: 8) if ctx else (lambda bi: bi)
    br_spec = pl.BlockSpec((1, tm, w), lambda bi, i: (bi, i, 0))
    gl_specs = [pl.BlockSpec((1, tm, d), lambda bi, i, n=n: (bi, i, 3 + n)) for n in range(3)]
    return pl.pallas_call(
        _merge_kernel,
        grid=(b, l // tm),
        in_specs=[br_spec, br_spec, br_spec] + gl_specs + [
            pl.BlockSpec((3, 1, d), lambda bi, i: (0, 0, 0)),
            pl.BlockSpec((3, w, d), lambda bi, i: (0, 0, 0)),
            pl.BlockSpec((d, d), lambda bi, i: (0, 0)),
            pl.BlockSpec((1, tm, d), lambda bi, i: (bi, i, 0)),
            pl.BlockSpec((None, None, 1, d), lambda bi, i: (row(bi), 2, 0, 0))],
        out_specs=pl.BlockSpec((1, tm, d), lambda bi, i: (bi, i, 0)),
        out_shape=jax.ShapeDtypeStruct((b, l, d), F32),
        compiler_params=_params(("arbitrary", "arbitrary")),
        name="merge",
    )(att, m, f, p, p, p, b_gate.reshape(3, 1, d), w_branch, w_out, x, mod4)


def _top_rows(s, k):
    n = s.shape[0]
    row = lax.broadcasted_iota(I32, s.shape, 0)
    vals, picks = [], []
    for _ in range(k):
        m = jnp.max(s, axis=0, keepdims=True)
        pos = jnp.min(jnp.where(s == m, row, n), axis=0, keepdims=True)
        vals.append(m)
        picks.append(pos)
        s = jnp.where(row == pos, -jnp.inf, s)
    return jnp.concatenate(vals, axis=0), jnp.concatenate(picks, axis=0)


def _candidates(s1, i1, s2, i2):
    k = s1.shape[0]
    ps, pi, first = [], [], []
    i = 0
    while k // (i + 1) > 1:
        n = k // (i + 1)
        for j0 in range(0, n, SUBLANES):
            v = s1[i:i + 1] + s2[j0:j0 + SUBLANES]
            if n < j0 + SUBLANES:
                v = jnp.where(lax.broadcasted_iota(I32, v.shape, 0) < n - j0, v, -jnp.inf)
            ps.append(v)
            pi.append(i1[i:i + 1] * PEER_KEYS + i2[j0:j0 + SUBLANES])
            first.append((i + 1) * (j0 + 1))
        i += 1
    assert k - i == SUBLANES
    ps.append(s1[i:] + s2[0:1])
    pi.append(i1[i:] * PEER_KEYS + i2[0:1])
    first.append(i + 1)
    return ps, pi, first


def _top_pairs(ps, pi, first, k):
    rows = [lax.broadcasted_iota(I32, p.shape, 0) + SUBLANES * n for n, p in enumerate(ps)]
    total = SUBLANES * len(ps)
    vals, picks = [], []
    for n in range(1, k + 1):
        act = [p for p in range(len(ps)) if first[p] <= n]
        m = jnp.max(functools.reduce(jnp.maximum, [ps[p] for p in act]), axis=0, keepdims=True)
        pos = jnp.min(functools.reduce(jnp.minimum, [jnp.where(ps[p] == m, rows[p], total) for p in act]),
                      axis=0, keepdims=True)
        hits = {p: rows[p] == pos for p in act}
        pick = functools.reduce(jnp.add, [jnp.where(hits[p], pi[p], 0) for p in act])
        vals.append(m)
        picks.append(jnp.sum(pick, axis=0, keepdims=True))
        ps = [jnp.where(hits[p], -jnp.inf, ps[p]) if p in hits else ps[p] for p in range(len(ps))]
    return jnp.concatenate(vals, axis=0), jnp.concatenate(picks, axis=0)


def _route_kernel(hq_ref, keys_ref, idx_ref, g_ref):
    half = PEER_KEYS
    picks, gates = [], []
    for h in range(PEER_HEADS):
        tops = []
        for part in range(2):
            c = (2 * h + part) * half
            qc = hq_ref[:, c:c + half]
            qn = qc * lax.rsqrt(jnp.mean(qc * qc, axis=-1, keepdims=True) + EPS)
            st = lax.dot_general(keys_ref[2 * h + part], qn.astype(BF16),
                                 (((1,), (1,)), ((), ())), preferred_element_type=F32)
            tops.append(_top_rows(st, PEER_TOPK))
        top_s, top_i = _top_pairs(*_candidates(*tops[0], *tops[1]), PEER_TOPK)
        e = jnp.exp(top_s - top_s[0:1])
        gates.append(e / jnp.sum(e, axis=0, keepdims=True))
        picks.append(top_i * ROW_WORDS)
    idx_ref[...] = jnp.concatenate(picks, axis=0).T
    g_ref[...] = jnp.concatenate(gates, axis=0).T


def peer_route(hq, keys):
    t, n = hq.shape
    tt = min(256, t)
    return pl.pallas_call(
        _route_kernel,
        grid=(t // tt,),
        in_specs=[pl.BlockSpec((tt, n), lambda i: (i, 0)),
                  pl.BlockSpec(keys.shape, lambda i: (0, 0, 0))],
        out_specs=[pl.BlockSpec((tt, PEER_SEL), lambda i: (i, 0)),
                   pl.BlockSpec((tt, PEER_SEL), lambda i: (i, 0))],
        out_shape=[jax.ShapeDtypeStruct((t, PEER_SEL), I32),
                   jax.ShapeDtypeStruct((t, PEER_SEL), F32)],
        compiler_params=_params(("arbitrary",)),
        name="peer_route",
    )(hq, keys)


def _chunk_tables():
    halves = 2
    per_expert = ROW_WORDS * halves
    col = jnp.arange(PEER_SEL * per_expert, dtype=I32)
    expand = (col[None, :] // per_expert == jnp.arange(PEER_SEL, dtype=I32)[:, None]).astype(BF16)
    chunk = (col % per_expert) // halves + ROW_WORDS * (col % halves)
    cmask = (chunk[None, :] == jnp.arange(SUBLANES, dtype=I32)[:, None]).astype(F32)
    return expand, cmask


def _fetch_rows(idx_ref, t, tab_ref, tile_ref, base):
    for k in range(PEER_SEL):
        i = pl.multiple_of(idx_ref[t, k], ROW_WORDS)
        tile_ref[base + ROW_WORDS * k:base + ROW_WORDS * (k + 1), :] = tab_ref[pl.ds(i, ROW_WORDS), :]


def _peer_u_kernel(idx_ref, h_ref, g_ref, gather_ref, cmask_ref, tab_ref, w_ref, tile_ref, z_ref):
    tt = h_ref.shape[0]
    slot_rows = PEER_SEL * ROW_WORDS
    cmask = cmask_ref[...]
    for t in range(tt):
        base = (t % PEER_U_SLOTS) * slot_rows
        _fetch_rows(idx_ref, t, tab_ref, tile_ref, base)
        rows = pltpu.bitcast(tile_ref[base:base + slot_rows, :], BF16)
        hb = jnp.concatenate([h_ref[t:t + 1, LANES * c:LANES * (c + 1)] for c in range(SUBLANES)],
                             axis=0).astype(BF16)
        r = lax.dot_general(hb, rows, (((1,), (1,)), ((), ())), preferred_element_type=F32)
        z_ref[t:t + 1, :] = jnp.sum(r * cmask, axis=0, keepdims=True)
    z = z_ref[...]
    zh = z.astype(BF16)
    zl = (z - zh.astype(F32)).astype(BF16)
    act = (jnp.dot(zh, gather_ref[...], preferred_element_type=F32)
           + jnp.dot(zl, gather_ref[...], preferred_element_type=F32))
    w_ref[...] = _gelu(act) * g_ref[...]


def peer_u(idx, h, g, tab):
    t, d = h.shape
    tt = PEER_TOKENS
    expand, cmask = _chunk_tables()
    gather = expand.T
    return pl.pallas_call(
        _peer_u_kernel,
        grid=(t // tt,),
        in_specs=[pl.BlockSpec((tt, PEER_SEL), lambda i: (i, 0), memory_space=pltpu.SMEM),
                  pl.BlockSpec((tt, d), lambda i: (i, 0)),
                  pl.BlockSpec((tt, PEER_SEL), lambda i: (i, 0)),
                  pl.BlockSpec(gather.shape, lambda i: (0, 0)),
                  pl.BlockSpec(cmask.shape, lambda i: (0, 0)),
                  pl.BlockSpec(tab.shape, lambda i: (0, 0), pipeline_mode=pl.Buffered(1))],
        out_specs=pl.BlockSpec((tt, PEER_SEL), lambda i: (i, 0)),
        out_shape=jax.ShapeDtypeStruct((t, PEER_SEL), F32),
        scratch_shapes=[pltpu.VMEM((PEER_U_SLOTS * PEER_SEL * ROW_WORDS, LANES), I32),
                        pltpu.VMEM((tt, PEER_SEL * SUBLANES), F32)],
        compiler_params=_params(("arbitrary",), 48 * 1024 * 1024),
        name="peer_u",
    )(idx, h, g, gather, cmask, tab)


def _peer_v_kernel(idx_ref, w_ref, x_ref, gate_ref, expand_ref, cmask_ref, tab_ref, o_ref, tile_ref):
    tt = x_ref.shape[0]
    slot_rows = PEER_SEL * ROW_WORDS
    wexp = jnp.dot(w_ref[...].astype(BF16), expand_ref[...], preferred_element_type=F32)
    cmask = cmask_ref[...]
    for t in range(tt):
        base = (t % PEER_V_SLOTS) * slot_rows
        _fetch_rows(idx_ref, t, tab_ref, tile_ref, base)
        rows = pltpu.bitcast(tile_ref[base:base + slot_rows, :], BF16)
        lhs = (wexp[t:t + 1, :] * cmask).astype(BF16)
        y = jnp.dot(lhs, rows, preferred_element_type=F32)
        for c in range(SUBLANES):
            cols = slice(LANES * c, LANES * (c + 1))
            o_ref[t:t + 1, cols] = x_ref[t:t + 1, cols] + gate_ref[:, cols] * y[c:c + 1, :]


def peer_v(idx, w, x, mod4, tab, *, ctx, seq):
    t, d = x.shape
    tt = PEER_TOKENS
    row = (lambda i: 8) if ctx else (lambda i: (i * tt) // seq)
    expand, cmask = _chunk_tables()
    return pl.pallas_call(
        _peer_v_kernel,
        grid=(t // tt,),
        in_specs=[pl.BlockSpec((tt, PEER_SEL), lambda i: (i, 0), memory_space=pltpu.SMEM),
                  pl.BlockSpec((tt, PEER_SEL), lambda i: (i, 0)),
                  pl.BlockSpec((tt, d), lambda i: (i, 0)),
                  pl.BlockSpec((None, None, 1, d), lambda i: (row(i), 5, 0, 0)),
                  pl.BlockSpec(expand.shape, lambda i: (0, 0)),
                  pl.BlockSpec(cmask.shape, lambda i: (0, 0)),
                  pl.BlockSpec(tab.shape, lambda i: (0, 0), pipeline_mode=pl.Buffered(1))],
        out_specs=pl.BlockSpec((tt, d), lambda i: (i, 0)),
        out_shape=jax.ShapeDtypeStruct(x.shape, F32),
        scratch_shapes=[pltpu.VMEM((PEER_V_SLOTS * PEER_SEL * ROW_WORDS, LANES), I32)],
        compiler_params=_params(("arbitrary",), 48 * 1024 * 1024),
        name="peer_v",
    )(idx, w, x, mod4, expand, cmask, tab)


def _pack_table(tab):
    e, d = tab.shape
    bits = lax.bitcast_convert_type(tab.astype(BF16), jnp.uint16).astype(jnp.uint32)
    word = bits[:, :d // 2] | (bits[:, d // 2:] << 16)
    return lax.bitcast_convert_type(word, I32).reshape(e * ROW_WORDS, LANES)


def _rope_tables(rows):
    r, col = jnp.meshgrid(jnp.arange(rows, dtype=F32), jnp.arange(GRID_W, dtype=F32), indexing="ij")
    freqs = ROPE_THETA ** (-jnp.arange(0, ROPE_AXIS_DIM, 2, dtype=F32) / ROPE_AXIS_DIM)
    ang_r = r.reshape(-1, 1) * freqs
    ang_c = col.reshape(-1, 1) * freqs
    cos = jnp.concatenate([jnp.cos(ang_r)] * 2 + [jnp.cos(ang_c)] * 2, axis=-1)
    sin = jnp.concatenate([-jnp.sin(ang_r), jnp.sin(ang_r), -jnp.sin(ang_c), jnp.sin(ang_c)], axis=-1)
    reps = HEADS * 2
    return jnp.tile(cos, (1, reps)), jnp.tile(sin, (1, reps))


def _dft(n):
    k = jnp.arange(n, dtype=I32)
    ang = ((k[:, None] * k[None, :]) % n).astype(F32) * (2.0 * math.pi / n)
    return jnp.cos(ang), jnp.sin(ang)


def _block_diag(m, groups):
    return jnp.kron(jnp.eye(groups, dtype=m.dtype), m)


def _peer(x, gain, mod4, w_q, keys, u_tab, v_tab, *, ctx):
    b, l, d = x.shape
    t = b * l
    h2, hq = normproj(x, gain, mod4, 3, 4, w_q, ctx=ctx, emit_h=True, out_dtype=F32)
    idx, g = peer_route(hq.reshape(t, -1), keys)
    w = peer_u(idx, h2.reshape(t, d), g, u_tab)
    out = peer_v(idx, w, x.reshape(t, d), mod4, v_tab, ctx=ctx, seq=l)
    return out.reshape(b, l, d)


def kernel(x, c, ctx, c_ctx, w_ada, b_ada, norm1_g, norm2_g, w_in, b_gate, q_norm_g, k_norm_g,
           lam_params, subln_g, cm_ln_g, cm_w_s, cm_b_s, w_branch, w_out,
           peer_w_q, peer_sub_keys, peer_u_tab, peer_v_tab):
    bsz, seq, d = x.shape
    lc = ctx.shape[1]
    depth = w_ada.shape[0]
    assert bsz <= 8 and d == SUBLANES * LANES and seq % GRID_W == 0

    cin = jnp.zeros((16, d), F32).at[:bsz].set(c).at[bsz].set(c_ctx)
    mods = ada_mod(cin, w_ada, b_ada)

    cos_l, sin_l = _rope_tables(seq // GRID_W)
    cos_c = jnp.ones((lc, cos_l.shape[1]), F32)
    sin_c = jnp.zeros((lc, cos_l.shape[1]), F32)
    gmean = _block_diag(jnp.full((HEAD_DIM, HEAD_DIM), 1.0 / HEAD_DIM, F32), HEADS * 2).astype(BF16)
    cl_l, sl_l = (m.astype(BF16) for m in _dft(seq))
    cl_c, sl_c = (m.astype(BF16) for m in _dft(lc))
    cg, sg = _dft(FN_GROUP_W)
    cc = _block_diag(cg, FN_GROUPS).astype(BF16)
    sc = _block_diag(sg, FN_GROUPS).astype(BF16)
    qscale = HEAD_DIM ** -0.5

    xc = ctx
    for l in range(depth):
        last = l == depth - 1
        lam_init = 0.8 - 0.6 * math.exp(-0.3 * l)
        mod4 = mods[l].reshape(16, 6, 1, d)
        w_in_l = w_in[l].astype(BF16)
        reps = HEADS * 2
        gains = jnp.stack([jnp.tile(q_norm_g[l], reps) * qscale, jnp.tile(k_norm_g[l], reps)])[:, None, :]
        gains_c = jnp.stack([jnp.tile(q_norm_g[l], reps) * qscale, jnp.tile(k_norm_g[l], reps)])[:, None, :]
        ws = cm_w_s[l].astype(BF16)
        bias2d = jnp.repeat(cm_b_s[l].T, BRANCH_W // CM_GROUPS, axis=1)
        wb = w_branch[l].astype(BF16)
        wo = w_out[l].astype(BF16)
        wq = peer_w_q[l].astype(BF16)
        keys = peer_sub_keys[l].reshape(PEER_HEADS * 2, PEER_KEYS, -1).astype(BF16)
        u_tab = _pack_table(peer_u_tab[l])
        v_tab = _pack_table(peer_v_tab[l])

        (p,) = normproj(x, norm1_g[l], mod4, 0, 1, w_in_l, ctx=False, emit_h=False, out_dtype=BF16)
        (pc,) = normproj(xc, norm1_g[l], mod4, 0, 1, w_in_l, ctx=True, emit_h=False, out_dtype=BF16)
        qk = qkprep(p, gains, cos_l, sin_l, gmean)
        qkc = qkprep(pc, gains_c, cos_c, sin_c, gmean)
        k_all = jnp.concatenate([qkc[1], qk[1]], axis=1)
        v_all = jnp.concatenate([pc[..., 1024:1536], p[..., 1024:1536]], axis=1)
        att = diff_attention(qk[0], k_all, v_all, lam_params[l], subln_g[l], lam_init)
        m = chunk_mlp(p, cm_ln_g[l], ws, bias2d)
        f = fourier_mix(p, cl_l, sl_l, cc, sc)
        x = merge(att, m, f, p, b_gate[l], wb, wo, x, mod4, ctx=False)
        if not last:
            attc = diff_attention(qkc[0], qkc[1], pc[..., 1024:1536], lam_params[l], subln_g[l], lam_init)
            mc = chunk_mlp(pc, cm_ln_g[l], ws, bias2d)
            fc = fourier_mix(pc, cl_c, sl_c, cc, sc)
            xc = merge(attc, mc, fc, pc, b_gate[l], wb, wo, xc, mod4, ctx=True)

        x = _peer(x, norm2_g[l], mod4, wq, keys, u_tab, v_tab, ctx=False)
        if not last:
            xc = _peer(xc, norm2_g[l], mod4, wq, keys, u_tab, v_tab, ctx=True)
    return x
```

```python
import functools
import math

import jax
import jax.numpy as jnp
from jax import lax
from jax.experimental import pallas as pl
from jax.experimental.pallas import tpu as pltpu

F32 = jnp.float32
BF16 = jnp.bfloat16
I32 = jnp.int32

LANES = 128
SUBLANES = 8
VMEM_BYTES = 64 * 1024 * 1024

EPS = 1e-6
GRID_W = 64
HEADS = 4
HEAD_DIM = 64
ROPE_AXIS_DIM = HEAD_DIM // 2
ROPE_THETA = 10000.0
BRANCH_W = 512
CM_CHUNK = 128
CM_GROUPS = 4
FN_GROUPS = 4
FN_GROUP_W = BRANCH_W // FN_GROUPS
PEER_HEADS = 8
PEER_KEYS = 128
PEER_TOPK = 16
PEER_SEL = PEER_HEADS * PEER_TOPK
ROW_WORDS = 4
PEER_TOKENS = 32
PEER_U_SLOTS = 4
PEER_V_SLOTS = 4


def _vmem_limit(nbytes):
    return int(min(VMEM_BYTES - 4 * 1024 * 1024, max(32 * 1024 * 1024, nbytes)))


def _params(sem, vmem=None):
    return pltpu.CompilerParams(dimension_semantics=sem,
                                vmem_limit_bytes=_vmem_limit(vmem or 0))


def _gelu(x):
    return 0.5 * x * (1.0 + lax.erf(x * (1.0 / math.sqrt(2.0))))


def _ada_kernel(c_ref, w_ref, b_ref, o_ref):
    c = c_ref[...]
    a = c * jax.nn.sigmoid(c)
    o_ref[0] = jnp.dot(a.astype(BF16), w_ref[0].astype(BF16), preferred_element_type=F32) + b_ref[0]


def ada_mod(cin, w_ada, b_ada):
    depth, d, n = w_ada.shape
    r = cin.shape[0]
    tn = 1536
    return pl.pallas_call(
        _ada_kernel,
        grid=(depth, n // tn),
        in_specs=[pl.BlockSpec((r, d), lambda l, j: (0, 0)),
                  pl.BlockSpec((1, d, tn), lambda l, j: (l, 0, j)),
                  pl.BlockSpec((1, 1, tn), lambda l, j: (l, 0, j))],
        out_specs=pl.BlockSpec((1, r, tn), lambda l, j: (l, 0, j)),
        out_shape=jax.ShapeDtypeStruct((depth, r, n), F32),
        compiler_params=_params(("arbitrary", "arbitrary")),
        name="ada_mod",
    )(cin, w_ada, b_ada.reshape(depth, 1, n))


def _normproj_kernel(x_ref, g_ref, sh_ref, sc_ref, w_ref, *refs, emit_h):
    if emit_h:
        h_ref, p_ref, hs_ref = refs
    else:
        p_ref, hs_ref = refs

    @pl.when(pl.program_id(2) == 0)
    def _():
        x = x_ref[0]
        y = x * lax.rsqrt(jnp.mean(x * x, axis=-1, keepdims=True) + EPS) * g_ref[...]
        h = y * (1.0 + sc_ref[...]) + sh_ref[...]
        hs_ref[...] = h.astype(BF16)
        if emit_h:
            h_ref[0] = h

    p_ref[0] = jnp.dot(hs_ref[...], w_ref[...], preferred_element_type=F32).astype(p_ref.dtype)


def normproj(x, gain, mod4, shift_i, scale_i, w, *, ctx, emit_h, out_dtype):
    b, l, d = x.shape
    n = w.shape[1]
    tm = min(1024, l)
    tn = 1536 if n % 1536 == 0 else 1024
    assert l % tm == 0 and n % tn == 0
    row = (lambda bi: 8) if ctx else (lambda bi: bi)
    in_specs = [pl.BlockSpec((1, tm, d), lambda bi, i, j: (bi, i, 0)),
                pl.BlockSpec((1, d), lambda bi, i, j: (0, 0)),
                pl.BlockSpec((None, None, 1, d), lambda bi, i, j: (row(bi), shift_i, 0, 0)),
                pl.BlockSpec((None, None, 1, d), lambda bi, i, j: (row(bi), scale_i, 0, 0)),
                pl.BlockSpec((d, tn), lambda bi, i, j: (0, j))]
    out_specs = [pl.BlockSpec((1, tm, tn), lambda bi, i, j: (bi, i, j))]
    out_shape = [jax.ShapeDtypeStruct((b, l, n), out_dtype)]
    if emit_h:
        out_specs = [pl.BlockSpec((1, tm, d), lambda bi, i, j: (bi, i, 0))] + out_specs
        out_shape = [jax.ShapeDtypeStruct((b, l, d), F32)] + out_shape
    return pl.pallas_call(
        functools.partial(_normproj_kernel, emit_h=emit_h),
        grid=(b, l // tm, n // tn),
        in_specs=in_specs, out_specs=out_specs, out_shape=out_shape,
        scratch_shapes=[pltpu.VMEM((tm, d), BF16)],
        compiler_params=_params(("arbitrary", "arbitrary", "arbitrary")),
        name="normproj",
    )(x, gain.reshape(1, d), mod4, mod4, w)


def _qkprep_kernel(p_ref, g_ref, cos_ref, sin_ref, gm_ref, o_ref):
    x = p_ref[0].astype(F32)
    ms = jnp.dot((x * x).astype(BF16), gm_ref[...], preferred_element_type=F32)
    y = x * lax.rsqrt(ms + EPS) * g_ref[0]
    w = y.shape[-1]
    lane = lax.broadcasted_iota(I32, y.shape, 1)
    half = ROPE_AXIS_DIM // 2
    partner = jnp.where((lane % ROPE_AXIS_DIM) < half,
                        pltpu.roll(y, w - half, axis=1), pltpu.roll(y, half, axis=1))
    o_ref[0, 0] = (y * cos_ref[...] + partner * sin_ref[...]).astype(o_ref.dtype)


def qkprep(p, gains, cos, sin, gmean):
    b, l, _ = p.shape
    w = HEADS * 2 * HEAD_DIM
    tm = min(512, l)
    return pl.pallas_call(
        _qkprep_kernel,
        grid=(2, b, l // tm),
        in_specs=[pl.BlockSpec((1, tm, w), lambda s, bi, i: (bi, i, s)),
                  pl.BlockSpec((1, 1, w), lambda s, bi, i: (s, 0, 0)),
                  pl.BlockSpec((tm, w), lambda s, bi, i: (i, 0)),
                  pl.BlockSpec((tm, w), lambda s, bi, i: (i, 0)),
                  pl.BlockSpec((w, w), lambda s, bi, i: (0, 0))],
        out_specs=pl.BlockSpec((1, 1, tm, w), lambda s, bi, i: (s, bi, i, 0)),
        out_shape=jax.ShapeDtypeStruct((2, b, l, w), BF16),
        compiler_params=_params(("arbitrary", "arbitrary", "arbitrary")),
        name="qkprep",
    )(p, gains, cos, sin, gmean)


def _attn_kernel(q_ref, k_ref, v_ref, lam_ref, g_ref, o_ref, *, lam_init):
    q = q_ref[0]
    k = k_ref[0]
    v = v_ref[0]
    lane = lax.broadcasted_iota(I32, q.shape, 1)
    zero = jnp.zeros_like(q)

    def softmax_v(qm):
        s = lax.dot_general(qm, k, (((1,), (1,)), ((), ())), preferred_element_type=F32)
        e = jnp.exp(s - jnp.max(s, axis=-1, keepdims=True))
        den = jnp.sum(e, axis=-1, keepdims=True)
        return jnp.dot(e.astype(BF16), v, preferred_element_type=F32) / den

    o1 = softmax_v(jnp.where(lane < HEAD_DIM, q, zero))
    o2 = softmax_v(jnp.where(lane >= HEAD_DIM, q, zero))
    lp = lam_ref[...]
    lam = (jnp.exp(jnp.sum(lp[0:1] * lp[1:2], axis=-1, keepdims=True))
           - jnp.exp(jnp.sum(lp[2:3] * lp[3:4], axis=-1, keepdims=True)) + lam_init)
    o = o1 - lam * o2
    y = o * lax.rsqrt(jnp.mean(o * o, axis=-1, keepdims=True) + EPS) * g_ref[...]
    o_ref[0] = (y * (1.0 - lam_init)).astype(o_ref.dtype)


def diff_attention(q, k, v, lam_params, subln_g, lam_init):
    b, lq, w = q.shape
    lk = k.shape[1]
    hw = w // HEADS
    tq = min(256, lq)
    return pl.pallas_call(
        functools.partial(_attn_kernel, lam_init=lam_init),
        grid=(b, HEADS, lq // tq),
        in_specs=[pl.BlockSpec((1, tq, hw), lambda bi, h, i: (bi, i, h)),
                  pl.BlockSpec((1, lk, hw), lambda bi, h, i: (bi, 0, h)),
                  pl.BlockSpec((1, lk, hw), lambda bi, h, i: (bi, 0, h)),
                  pl.BlockSpec((4, HEAD_DIM), lambda bi, h, i: (0, 0)),
                  pl.BlockSpec((1, hw), lambda bi, h, i: (0, 0))],
        out_specs=pl.BlockSpec((1, tq, hw), lambda bi, h, i: (bi, i, h)),
        out_shape=jax.ShapeDtypeStruct((b, lq, w), BF16),
        compiler_params=_params(("arbitrary", "arbitrary", "arbitrary"), 48 * 1024 * 1024),
        name="diff_attention",
    )(q, k, v, lam_params, subln_g.reshape(1, hw))


def _chunkmlp_kernel(zu_ref, zv_ref, g_ref, ws_ref, bias_ref, o_ref):
    u = _gelu(zu_ref[0].astype(F32))
    gv = _gelu(zv_ref[0].astype(F32))
    mu = jnp.mean(gv, axis=-1, keepdims=True)
    var = jnp.mean(jnp.square(gv - mu), axis=-1, keepdims=True)
    vn = ((gv - mu) * lax.rsqrt(var + EPS) * g_ref[...]).astype(BF16)
    tm = u.shape[0]
    gw = BRANCH_W // CM_GROUPS
    for n in range(tm // CM_CHUNK):
        rows = slice(n * CM_CHUNK, (n + 1) * CM_CHUNK)
        parts = [jnp.dot(ws_ref[g], vn[rows, g * gw:(g + 1) * gw], preferred_element_type=F32)
                 for g in range(CM_GROUPS)]
        s = jnp.concatenate(parts, axis=-1) + bias_ref[...]
        o_ref[0, rows, :] = (u[rows] * s).astype(o_ref.dtype)


def chunk_mlp(p, ln_g, w_s, bias2d):
    b, l, _ = p.shape
    w = BRANCH_W
    tm = min(512, l)
    return pl.pallas_call(
        _chunkmlp_kernel,
        grid=(b, l // tm),
        in_specs=[pl.BlockSpec((1, tm, w), lambda bi, i: (bi, i, 3)),
                  pl.BlockSpec((1, tm, w), lambda bi, i: (bi, i, 4)),
                  pl.BlockSpec((1, w), lambda bi, i: (0, 0)),
                  pl.BlockSpec((CM_GROUPS, CM_CHUNK, CM_CHUNK), lambda bi, i: (0, 0, 0)),
                  pl.BlockSpec((CM_CHUNK, w), lambda bi, i: (0, 0))],
        out_specs=pl.BlockSpec((1, tm, w), lambda bi, i: (bi, i, 0)),
        out_shape=jax.ShapeDtypeStruct((b, l, w), BF16),
        compiler_params=_params(("arbitrary", "arbitrary")),
        name="chunk_mlp",
    )(p, p, ln_g.reshape(1, w), w_s, bias2d)


def _fourier_kernel(z_ref, cl_ref, sl_ref, cc_ref, sc_ref, o_ref, y1_ref, y2_ref, *, scale):
    @pl.when(pl.program_id(1) == 0)
    def _():
        z = z_ref[0]
        y1_ref[...] = jnp.dot(z, cc_ref[...], preferred_element_type=F32).astype(BF16)
        y2_ref[...] = jnp.dot(z, sc_ref[...], preferred_element_type=F32).astype(BF16)

    acc = (jnp.dot(cl_ref[...], y1_ref[...], preferred_element_type=F32)
           - jnp.dot(sl_ref[...], y2_ref[...], preferred_element_type=F32))
    o_ref[0] = (acc * scale).astype(o_ref.dtype)


def fourier_mix(p, cl, sl, cc, sc):
    b, l, _ = p.shape
    w = BRANCH_W
    tm = min(512, l)
    scale = 1.0 / math.sqrt(l * FN_GROUP_W)
    return pl.pallas_call(
        functools.partial(_fourier_kernel, scale=scale),
        grid=(b, l // tm),
        in_specs=[pl.BlockSpec((1, l, w), lambda bi, i: (bi, 0, 5)),
                  pl.BlockSpec((tm, l), lambda bi, i: (i, 0)),
                  pl.BlockSpec((tm, l), lambda bi, i: (i, 0)),
                  pl.BlockSpec((w, w), lambda bi, i: (0, 0)),
                  pl.BlockSpec((w, w), lambda bi, i: (0, 0))],
        out_specs=pl.BlockSpec((1, tm, w), lambda bi, i: (bi, i, 0)),
        out_shape=jax.ShapeDtypeStruct((b, l, w), BF16),
        scratch_shapes=[pltpu.VMEM((l, w), BF16), pltpu.VMEM((l, w), BF16)],
        compiler_params=_params(("arbitrary", "arbitrary"), 48 * 1024 * 1024),
        name="fourier_mix",
    )(p, cl, sl, cc, sc)


def _merge_kernel(a_ref, m_ref, f_ref, gl0_ref, gl1_ref, gl2_ref, bg_ref, wb_ref, wo_ref,
                  x_ref, gate_ref, o_ref):
    s = None
    for n, (br, gl) in enumerate(((a_ref, gl0_ref), (m_ref, gl1_ref), (f_ref, gl2_ref))):
        y = jnp.dot(br[0], wb_ref[n], preferred_element_type=F32)
        t = jax.nn.sigmoid(gl[0].astype(F32) + bg_ref[n]) * y
        s = t if s is None else s + t
    y = jnp.dot(s.astype(BF16), wo_ref[...], preferred_element_type=F32)
    o_ref[0] = x_ref[0] + gate_ref[...] * y


def merge(att, m, f, p, b_gate, w_branch, w_out, x, mod4, *, ctx):
    b, l, d = x.shape
    w = BRANCH_W
    tm = min(256, l)
    row = (lambda bi: 8) if ctx else (lambda bi: bi)
    br_spec = pl.BlockSpec((1, tm, w), lambda bi, i: (bi, i, 0))
    gl_specs = [pl.BlockSpec((1, tm, d), lambda bi, i, n=n: (bi, i, 3 + n)) for n in range(3)]
    return pl.pallas_call(
        _merge_kernel,
        grid=(b, l // tm),
        in_specs=[br_spec, br_spec, br_spec] + gl_specs + [
            pl.BlockSpec((3, 1, d), lambda bi, i: (0, 0, 0)),
            pl.BlockSpec((3, w, d), lambda bi, i: (0, 0, 0)),
            pl.BlockSpec((d, d), lambda bi, i: (0, 0)),
            pl.BlockSpec((1, tm, d), lambda bi, i: (bi, i, 0)),
            pl.BlockSpec((None, None, 1, d), lambda bi, i: (row(bi), 2, 0, 0))],
        out_specs=pl.BlockSpec((1, tm, d), lambda bi, i: (bi, i, 0)),
        out_shape=jax.ShapeDtypeStruct((b, l, d), F32),
        compiler_params=_params(("arbitrary", "arbitrary")),
        name="merge",
    )(att, m, f, p, p, p, b_gate.reshape(3, 1, d), w_branch, w_out, x, mod4)


def _top_rows(s, k):
    n = s.shape[0]
    row = lax.broadcasted_iota(I32, s.shape, 0)
    vals, picks = [], []
    for _ in range(k):
        m = jnp.max(s, axis=0, keepdims=True)
        pos = jnp.min(jnp.where(s == m, row, n), axis=0, keepdims=True)
        vals.append(m)
        picks.append(pos)
        s = jnp.where(row == pos, -jnp.inf, s)
    return jnp.concatenate(vals, axis=0), jnp.concatenate(picks, axis=0)


def _candidates(s1, i1, s2, i2):
    k = s1.shape[0]
    ps, pi, first = [], [], []
    i = 0
    while k // (i + 1) > 1:
        n = k // (i + 1)
        for j0 in range(0, n, SUBLANES):
            v = s1[i:i + 1] + s2[j0:j0 + SUBLANES]
            if n < j0 + SUBLANES:
                v = jnp.where(lax.broadcasted_iota(I32, v.shape, 0) < n - j0, v, -jnp.inf)
            ps.append(v)
            pi.append(i1[i:i + 1] * PEER_KEYS + i2[j0:j0 + SUBLANES])
            first.append((i + 1) * (j0 + 1))
        i += 1
    assert k - i == SUBLANES
    ps.append(s1[i:] + s2[0:1])
    pi.append(i1[i:] * PEER_KEYS + i2[0:1])
    first.append(i + 1)
    return ps, pi, first


def _top_pairs(ps, pi, first, k):
    rows = [lax.broadcasted_iota(I32, p.shape, 0) + SUBLANES * n for n, p in enumerate(ps)]
    total = SUBLANES * len(ps)
    vals, picks = [], []
    for n in range(1, k + 1):
        act = [p for p in range(len(ps)) if first[p] <= n]
        m = jnp.max(functools.reduce(jnp.maximum, [ps[p] for p in act]), axis=0, keepdims=True)
        pos = jnp.min(functools.reduce(jnp.minimum, [jnp.where(ps[p] == m, rows[p], total) for p in act]),
                      axis=0, keepdims=True)
        hits = {p: rows[p] == pos for p in act}
        pick = functools.reduce(jnp.add, [jnp.where(hits[p], pi[p], 0) for p in act])
        vals.append(m)
        picks.append(jnp.sum(pick, axis=0, keepdims=True))
        ps = [jnp.where(hits[p], -jnp.inf, ps[p]) if p in hits else ps[p] for p in range(len(ps))]
    return jnp.concatenate(vals, axis=0), jnp.concatenate(picks, axis=0)


def _route_kernel(hq_ref, keys_ref, idx_ref, g_ref):
    half = PEER_KEYS
    picks, gates = [], []
    for h in range(PEER_HEADS):
        tops = []
        for part in range(2):
            c = (2 * h + part) * half
            qc = hq_ref[:, c:c + half]
            qn = qc * lax.rsqrt(jnp.mean(qc * qc, axis=-1, keepdims=True) + EPS)
            st = lax.dot_general(keys_ref[2 * h + part], qn.astype(BF16),
                                 (((1,), (1,)), ((), ())), preferred_element_type=F32)
            tops.append(_top_rows(st, PEER_TOPK))
        top_s, top_i = _top_pairs(*_candidates(*tops[0], *tops[1]), PEER_TOPK)
        e = jnp.exp(top_s - top_s[0:1])
        gates.append(e / jnp.sum(e, axis=0, keepdims=True))
        picks.append(top_i * ROW_WORDS)
    idx_ref[...] = jnp.concatenate(picks, axis=0).T
    g_ref[...] = jnp.concatenate(gates, axis=0).T


def peer_route(hq, keys):
    t, n = hq.shape
    tt = min(256, t)
    return pl.pallas_call(
        _route_kernel,
        grid=(t // tt,),
        in_specs=[pl.BlockSpec((tt, n), lambda i: (i, 0)),
                  pl.BlockSpec(keys.shape, lambda i: (0, 0, 0))],
        out_specs=[pl.BlockSpec((tt, PEER_SEL), lambda i: (i, 0)),
                   pl.BlockSpec((tt, PEER_SEL), lambda i: (i, 0))],
        out_shape=[jax.ShapeDtypeStruct((t, PEER_SEL), I32),
                   jax.ShapeDtypeStruct((t, PEER_SEL), F32)],
        compiler_params=_params(("arbitrary",)),
        name="peer_route",
    )(hq, keys)


def _chunk_tables():
    halves = 2
    per_expert = ROW_WORDS * halves
    col = jnp.arange(PEER_SEL * per_expert, dtype=I32)
    expand = (col[None, :] // per_expert == jnp.arange(PEER_SEL, dtype=I32)[:, None]).astype(BF16)
    chunk = (col % per_expert) // halves + ROW_WORDS * (col % halves)
    cmask = (chunk[None, :] == jnp.arange(SUBLANES, dtype=I32)[:, None]).astype(F32)
    return expand, cmask


def _fetch_rows(idx_ref, t, tab_ref, tile_ref, base):
    for k in range(PEER_SEL):
        i = pl.multiple_of(idx_ref[t, k], ROW_WORDS)
        tile_ref[base + ROW_WORDS * k:base + ROW_WORDS * (k + 1), :] = tab_ref[pl.ds(i, ROW_WORDS), :]


def _peer_u_kernel(idx_ref, h_ref, g_ref, gather_ref, cmask_ref, tab_ref, w_ref, tile_ref, z_ref):
    tt = h_ref.shape[0]
    slot_rows = PEER_SEL * ROW_WORDS
    cmask = cmask_ref[...]
    for t in range(tt):
        base = (t % PEER_U_SLOTS) * slot_rows
        _fetch_rows(idx_ref, t, tab_ref, tile_ref, base)
        rows = pltpu.bitcast(tile_ref[base:base + slot_rows, :], BF16)
        hb = jnp.concatenate([h_ref[t:t + 1, LANES * c:LANES * (c + 1)] for c in range(SUBLANES)],
                             axis=0).astype(BF16)
        r = lax.dot_general(hb, rows, (((1,), (1,)), ((), ())), preferred_element_type=F32)
        z_ref[t:t + 1, :] = jnp.sum(r * cmask, axis=0, keepdims=True)
    z = z_ref[...]
    zh = z.astype(BF16)
    zl = (z - zh.astype(F32)).astype(BF16)
    act = (jnp.dot(zh, gather_ref[...], preferred_element_type=F32)
           + jnp.dot(zl, gather_ref[...], preferred_element_type=F32))
    w_ref[...] = _gelu(act) * g_ref[...]


def peer_u(idx, h, g, tab):
    t, d = h.shape
    tt = PEER_TOKENS
    expand, cmask = _chunk_tables()
    gather = expand.T
    return pl.pallas_call(
        _peer_u_kernel,
        grid=(t // tt,),
        in_specs=[pl.BlockSpec((tt, PEER_SEL), lambda i: (i, 0), memory_space=pltpu.SMEM),
                  pl.BlockSpec((tt, d), lambda i: (i, 0)),
                  pl.BlockSpec((tt, PEER_SEL), lambda i: (i, 0)),
                  pl.BlockSpec(gather.shape, lambda i: (0, 0)),
                  pl.BlockSpec(cmask.shape, lambda i: (0, 0)),
                  pl.BlockSpec(tab.shape, lambda i: (0, 0), pipeline_mode=pl.Buffered(1))],
        out_specs=pl.BlockSpec((tt, PEER_SEL), lambda i: (i, 0)),
        out_shape=jax.ShapeDtypeStruct((t, PEER_SEL), F32),
        scratch_shapes=[pltpu.VMEM((PEER_U_SLOTS * PEER_SEL * ROW_WORDS, LANES), I32),
                        pltpu.VMEM((tt, PEER_SEL * SUBLANES), F32)],
        compiler_params=_params(("arbitrary",), 48 * 1024 * 1024),
        name="peer_u",
    )(idx, h, g, gather, cmask, tab)


def _peer_v_kernel(idx_ref, w_ref, x_ref, gate_ref, expand_ref, cmask_ref, tab_ref, o_ref, tile_ref):
    tt = x_ref.shape[0]
    slot_rows = PEER_SEL * ROW_WORDS
    wexp = jnp.dot(w_ref[...].astype(BF16), expand_ref[...], preferred_element_type=F32)
    cmask = cmask_ref[...]
    for t in range(tt):
        base = (t % PEER_V_SLOTS) * slot_rows
        _fetch_rows(idx_ref, t, tab_ref, tile_ref, base)
        rows = pltpu.bitcast(tile_ref[base:base + slot_rows, :], BF16)
        lhs = (wexp[t:t + 1, :] * cmask).astype(BF16)
        y = jnp.dot(lhs, rows, preferred_element_type=F32)
        for c in range(SUBLANES):
            cols = slice(LANES * c, LANES * (c + 1))
            o_ref[t:t + 1, cols] = x_ref[t:t + 1, cols] + gate_ref[:, cols] * y[c:c + 1, :]


def peer_v(idx, w, x, mod4, tab, *, ctx, seq):
    t, d = x.shape
    tt = PEER_TOKENS
    row = (lambda i: 8) if ctx else (lambda i: (i * tt) // seq)
    expand, cmask = _chunk_tables()
    return pl.pallas_call(
        _peer_v_kernel,
        grid=(t // tt,),
        in_specs=[pl.BlockSpec((tt, PEER_SEL), lambda i: (i, 0), memory_space=pltpu.SMEM),
                  pl.BlockSpec((tt, PEER_SEL), lambda i: (i, 0)),
                  pl.BlockSpec((tt, d), lambda i: (i, 0)),
                  pl.BlockSpec((None, None, 1, d), lambda i: (row(i), 5, 0, 0)),
                  pl.BlockSpec(expand.shape, lambda i: (0, 0)),
                  pl.BlockSpec(cmask.shape, lambda i: (0, 0)),
                  pl.BlockSpec(tab.shape, lambda i: (0, 0), pipeline_mode=pl.Buffered(1))],
        out_specs=pl.BlockSpec((tt, d), lambda i: (i, 0)),
        out_shape=jax.ShapeDtypeStruct(x.shape, F32),
        scratch_shapes=[pltpu.VMEM((PEER_V_SLOTS * PEER_SEL * ROW_WORDS, LANES), I32)],
        compiler_params=_params(("arbitrary",), 48 * 1024 * 1024),
        name="peer_v",
    )(idx, w, x, mod4, expand, cmask, tab)


def _pack_table(tab):
    e, d = tab.shape
    bits = lax.bitcast_convert_type(tab.astype(BF16), jnp.uint16).astype(jnp.uint32)
    word = bits[:, :d // 2] | (bits[:, d // 2:] << 16)
    return lax.bitcast_convert_type(word, I32).reshape(e * ROW_WORDS, LANES)


def _rope_tables(rows):
    r, col = jnp.meshgrid(jnp.arange(rows, dtype=F32), jnp.arange(GRID_W, dtype=F32), indexing="ij")
    freqs = ROPE_THETA ** (-jnp.arange(0, ROPE_AXIS_DIM, 2, dtype=F32) / ROPE_AXIS_DIM)
    ang_r = r.reshape(-1, 1) * freqs
    ang_c = col.reshape(-1, 1) * freqs
    cos = jnp.concatenate([jnp.cos(ang_r)] * 2 + [jnp.cos(ang_c)] * 2, axis=-1)
    sin = jnp.concatenate([-jnp.sin(ang_r), jnp.sin(ang_r), -jnp.sin(ang_c), jnp.sin(ang_c)], axis=-1)
    reps = HEADS * 2
    return jnp.tile(cos, (1, reps)), jnp.tile(sin, (1, reps))


def _dft(n):
    k = jnp.arange(n, dtype=I32)
    ang = ((k[:, None] * k[None, :]) % n).astype(F32) * (2.0 * math.pi / n)
    return jnp.cos(ang), jnp.sin(ang)


def _block_diag(m, groups):
    return jnp.kron(jnp.eye(groups, dtype=m.dtype), m)


def _peer(x, gain, mod4, w_q, keys, u_tab, v_tab, *, ctx):
    b, l, d = x.shape
    t = b * l
    h2, hq = normproj(x, gain, mod4, 3, 4, w_q, ctx=ctx, emit_h=True, out_dtype=F32)
    idx, g = peer_route(hq.reshape(t, -1), keys)
    w = peer_u(idx, h2.reshape(t, d), g, u_tab)
    out = peer_v(idx, w, x.reshape(t, d), mod4, v_tab, ctx=ctx, seq=l)
    return out.reshape(b, l, d)


def kernel(x, c, ctx, c_ctx, w_ada, b_ada, norm1_g, norm2_g, w_in, b_gate, q_norm_g, k_norm_g,
           lam_params, subln_g, cm_ln_g, cm_w_s, cm_b_s, w_branch, w_out,
           peer_w_q, peer_sub_keys, peer_u_tab, peer_v_tab):
    bsz, seq, d = x.shape
    lc = ctx.shape[1]
    depth = w_ada.shape[0]
    assert bsz <= 8 and d == SUBLANES * LANES and seq % GRID_W == 0

    cin = jnp.zeros((16, d), F32).at[:bsz].set(c).at[bsz].set(c_ctx)
    mods = ada_mod(cin, w_ada, b_ada)

    cos_l, sin_l = _rope_tables(seq // GRID_W)
    cos_c = jnp.ones((lc, cos_l.shape[1]), F32)
    sin_c = jnp.zeros((lc, cos_l.shape[1]), F32)
    gmean = _block_diag(jnp.full((HEAD_DIM, HEAD_DIM), 1.0 / HEAD_DIM, F32), HEADS * 2).astype(BF16)
    cl_l, sl_l = (m.astype(BF16) for m in _dft(seq))
    cl_c, sl_c = (m.astype(BF16) for m in _dft(lc))
    cg, sg = _dft(FN_GROUP_W)
    cc = _block_diag(cg, FN_GROUPS).astype(BF16)
    sc = _block_diag(sg, FN_GROUPS).astype(BF16)
    qscale = HEAD_DIM ** -0.5

    xc = ctx
    for l in range(depth):
        last = l == depth - 1
        lam_init = 0.8 - 0.6 * math.exp(-0.3 * l)
        mod4 = mods[l].reshape(16, 6, 1, d)
        w_in_l = w_in[l].astype(BF16)
        reps = HEADS * 2
        gains = jnp.stack([jnp.tile(q_norm_g[l], reps) * qscale, jnp.tile(k_norm_g[l], reps)])[:, None, :]
        gains_c = jnp.stack([jnp.tile(q_norm_g[l], reps) * qscale, jnp.tile(k_norm_g[l], reps)])[:, None, :]
        ws = cm_w_s[l].astype(BF16)
        bias2d = jnp.repeat(cm_b_s[l].T, BRANCH_W // CM_GROUPS, axis=1)
        wb = w_branch[l].astype(BF16)
        wo = w_out[l].astype(BF16)
        wq = peer_w_q[l].astype(BF16)
        keys = peer_sub_keys[l].reshape(PEER_HEADS * 2, PEER_KEYS, -1).astype(BF16)
        u_tab = _pack_table(peer_u_tab[l])
        v_tab = _pack_table(peer_v_tab[l])

        (p,) = normproj(x, norm1_g[l], mod4, 0, 1, w_in_l, ctx=False, emit_h=False, out_dtype=BF16)
        (pc,) = normproj(xc, norm1_g[l], mod4, 0, 1, w_in_l, ctx=True, emit_h=False, out_dtype=BF16)
        qk = qkprep(p, gains, cos_l, sin_l, gmean)
        qkc = qkprep(pc, gains_c, cos_c, sin_c, gmean)
        k_all = jnp.concatenate([qkc[1], qk[1]], axis=1)
        v_all = jnp.concatenate([pc[..., 1024:1536], p[..., 1024:1536]], axis=1)
        att = diff_attention(qk[0], k_all, v_all, lam_params[l], subln_g[l], lam_init)
        m = chunk_mlp(p, cm_ln_g[l], ws, bias2d)
        f = fourier_mix(p, cl_l, sl_l, cc, sc)
        x = merge(att, m, f, p, b_gate[l], wb, wo, x, mod4, ctx=False)
        if not last:
            attc = diff_attention(qkc[0], qkc[1], pc[..., 1024:1536], lam_params[l], subln_g[l], lam_init)
            mc = chunk_mlp(pc, cm_ln_g[l], ws, bias2d)
            fc = fourier_mix(pc, cl_c, sl_c, cc, sc)
            xc = merge(attc, mc, fc, pc, b_gate[l], wb, wo, xc, mod4, ctx=True)

        x = _peer(x, norm2_g[l], mod4, wq, keys, u_tab, v_tab, ctx=False)
        if not last:
            xc = _peer(xc, norm2_g[l], mod4, wq, keys, u_tab, v_tab, ctx=True)
    return x
```
